```python
import math
import jax, jax.numpy as jnp
from jax import lax
import numpy as np


D_MODEL = 4096
BATCH = 4
SEQ = 4096
DEPTH = 2

N_MEM = 256
D_MIX = D_MODEL
W_POOL = D_MIX // 2
POOL_WINDOWS = (2, 4, 8, 16)
N_POOL_GROUPS = len(POOL_WINDOWS)
POOL_GROUP = W_POOL // N_POOL_GROUPS
W_SGU = D_MIX - W_POOL
SGU_HEAD = 128
N_SGU_HEADS = W_SGU // SGU_HEAD
CHUNK = 128
D_IN = W_POOL + 2 * W_SGU
N_XHEADS = 4
XHEAD_DIM = D_MODEL // N_XHEADS
D_FF = 4 * D_MODEL
EPS = 1e-6

kernel_name = "hybrid_pool_sgu_memxattn_trunk"


def rmsnorm(x, g):
    xf = x.astype(jnp.float32)
    y = xf * lax.rsqrt(jnp.mean(xf * xf, axis=-1, keepdims=True) + EPS)
    return (y * g.astype(jnp.float32)).astype(x.dtype)


def layernorm(x, g, b):
    xf = x.astype(jnp.float32)
    mu = jnp.mean(xf, axis=-1, keepdims=True)
    var = jnp.mean(jnp.square(xf - mu), axis=-1, keepdims=True)
    y = (xf - mu) * lax.rsqrt(var + EPS)
    return (y * g.astype(jnp.float32) + b.astype(jnp.float32)).astype(x.dtype)


def pool_mixer(a, pool_w, pool_scale):
    b, s, _ = a.shape
    ag = a.reshape(b, s, N_POOL_GROUPS, POOL_GROUP)
    pos_count = jnp.arange(1, s + 1, dtype=jnp.float32)
    diffs = []
    for gi, w in enumerate(POOL_WINDOWS):
        xg = ag[:, :, gi].astype(jnp.float32)
        cs = jnp.cumsum(xg, axis=1)
        lag = jnp.pad(cs, ((0, 0), (w, 0), (0, 0)))[:, :s]
        cnt = jnp.minimum(pos_count, jnp.float32(w))[None, :, None]
        diffs.append((cs - lag) / cnt - xg)
    d = jnp.stack(diffs, axis=2).astype(a.dtype)
    y = jnp.einsum('bsgc,gcd->bsgd', d, pool_w).reshape(b, s, W_POOL)
    return y * pool_scale


def sgu_mixer(zu, zv, ln_g, ln_b, w_s, b_s):
    b, s, _ = zu.shape
    u = jax.nn.gelu(zu)
    v = layernorm(jax.nn.gelu(zv), ln_g, ln_b)
    n_chunks = s // CHUNK
    v = v.reshape(b, n_chunks, CHUNK, N_SGU_HEADS, SGU_HEAD)
    causal = jnp.tril(jnp.ones((CHUNK, CHUNK), dtype=w_s.dtype))
    wm = w_s * causal[None]
    mixed = jnp.einsum('hts,bnshd->bnthd', wm, v) + jnp.transpose(b_s)[None, None, :, :, None]
    return u * mixed.reshape(b, s, W_SGU)


def cross_attention(h, m, w_q, w_k, w_v, w_o):
    b, s, _ = h.shape
    nm = m.shape[1]
    q = (h @ w_q).reshape(b, s, N_XHEADS, XHEAD_DIM)
    k = (m @ w_k).reshape(b, nm, N_XHEADS, XHEAD_DIM)
    v = (m @ w_v).reshape(b, nm, N_XHEADS, XHEAD_DIM)
    scores = jnp.einsum('bshd,bmhd->bhsm', q, k).astype(jnp.float32) * (XHEAD_DIM ** -0.5)
    p = jax.nn.softmax(scores, axis=-1).astype(h.dtype)
    o = jnp.einsum('bhsm,bmhd->bshd', p, v).reshape(b, s, D_MODEL)
    return o @ w_o


def setup_inputs(seed: int = 0) -> dict:
    key = jax.random.key(seed)
    ks = jax.random.split(key, 24)
    f32 = jnp.float32

    def nrm(k, shape, scale):
        return jax.random.normal(k, shape, f32) * scale

    def gain(k, shape):
        return 1.0 + 0.02 * jax.random.normal(k, shape, f32)

    return {
        "x": jax.random.normal(ks[0], (BATCH, SEQ, D_MODEL), f32),
        "mem": jax.random.normal(ks[1], (BATCH, N_MEM, D_MODEL), f32),
        "ln_mix": gain(ks[2], (DEPTH, D_MODEL)),
        "w_in": nrm(ks[3], (DEPTH, D_MODEL, D_IN), D_MODEL ** -0.5),
        "pool_w": nrm(ks[4], (DEPTH, N_POOL_GROUPS, POOL_GROUP, POOL_GROUP), POOL_GROUP ** -0.5),
        "pool_scale": gain(ks[5], (DEPTH, W_POOL)),
        "sgu_ln_g": gain(ks[6], (DEPTH, W_SGU)),
        "sgu_ln_b": nrm(ks[7], (DEPTH, W_SGU), 0.02),
        "sgu_w": nrm(ks[8], (DEPTH, N_SGU_HEADS, CHUNK, CHUNK), 0.5 * CHUNK ** -0.5),
        "sgu_b": gain(ks[9], (DEPTH, N_SGU_HEADS, CHUNK)),
        "w_out": nrm(ks[10], (DEPTH, D_MIX, D_MODEL), D_MIX ** -0.5),
        "ln_x": gain(ks[11], (DEPTH, D_MODEL)),
        "ln_mem": gain(ks[12], (DEPTH, D_MODEL)),
        "w_q": nrm(ks[13], (DEPTH, D_MODEL, D_MODEL), D_MODEL ** -0.5),
        "w_k": nrm(ks[14], (DEPTH, D_MODEL, D_MODEL), D_MODEL ** -0.5),
        "w_v": nrm(ks[15], (DEPTH, D_MODEL, D_MODEL), D_MODEL ** -0.5),
        "w_o": nrm(ks[16], (DEPTH, D_MODEL, D_MODEL), D_MODEL ** -0.5),
        "ln_ffn": gain(ks[17], (DEPTH, D_MODEL)),
        "w_up": nrm(ks[18], (DEPTH, D_MODEL, D_FF), D_MODEL ** -0.5),
        "w_down": nrm(ks[19], (DEPTH, D_FF, D_MODEL), D_FF ** -0.5),
        "ln_final": gain(ks[20], (D_MODEL,)),
    }


def reference(x, mem, ln_mix, w_in, pool_w, pool_scale, sgu_ln_g, sgu_ln_b, sgu_w, sgu_b,
              w_out, ln_x, ln_mem, w_q, w_k, w_v, w_o, ln_ffn, w_up, w_down, ln_final):
    for l in range(DEPTH):
        h = rmsnorm(x, ln_mix[l])
        z = h @ w_in[l]
        z_pool = z[..., :W_POOL]
        z_u = z[..., W_POOL:W_POOL + W_SGU]
        z_v = z[..., W_POOL + W_SGU:]
        a = pool_mixer(z_pool, pool_w[l], pool_scale[l])
        g = sgu_mixer(z_u, z_v, sgu_ln_g[l], sgu_ln_b[l], sgu_w[l], sgu_b[l])
        x = x + jnp.concatenate([a, g], axis=-1) @ w_out[l]
        hx = rmsnorm(x, ln_x[l])
        mn = rmsnorm(mem, ln_mem[l])
        x = x + cross_attention(hx, mn, w_q[l], w_k[l], w_v[l], w_o[l])
        hf = rmsnorm(x, ln_ffn[l])
        x = x + jnp.square(jax.nn.relu(hf @ w_up[l])) @ w_down[l]
    return rmsnorm(x, ln_final)
```

```python
import functools

import jax
import jax.numpy as jnp
from jax import lax
from jax.experimental import pallas as pl
from jax.experimental.pallas import tpu as pltpu

EPS = 1e-6
POOL_WINDOWS = (2, 4, 8, 16)
POOL_HALO = 16
CHUNK = 128
SGU_HEAD = 128
N_XHEADS = 4

V7X_VMEM_LIMIT_BYTES = 56 * 1024 * 1024

MM_BM = 1024
MM_BN = 1024
MM_BK = 2048
ROW_TILE = 256

F32 = jnp.float32
BF16 = jnp.bfloat16


def _tile(dim, pref):
    t = min(dim, pref)
    assert dim % t == 0, (dim, pref)
    return t


def _params(*sem):
    return pltpu.CompilerParams(dimension_semantics=sem,
                                vmem_limit_bytes=V7X_VMEM_LIMIT_BYTES)


def _prep_kernel(x_ref, g_ref, xb_ref, r_ref):
    x = x_ref[...]
    r_ref[...] = lax.rsqrt(jnp.mean(x * x, axis=-1, keepdims=True) + EPS)
    xb_ref[...] = (x * g_ref[...]).astype(BF16)


def _prep(x, g):
    m, d = x.shape
    bm = _tile(m, ROW_TILE)
    return pl.pallas_call(
        _prep_kernel,
        grid=(m // bm,),
        in_specs=[pl.BlockSpec((bm, d), lambda i: (i, 0)),
                  pl.BlockSpec((1, d), lambda i: (0, 0))],
        out_specs=[pl.BlockSpec((bm, d), lambda i: (i, 0)),
                   pl.BlockSpec((bm, 1), lambda i: (i, 0))],
        out_shape=[jax.ShapeDtypeStruct((m, d), BF16),
                   jax.ShapeDtypeStruct((m, 1), F32)],
        compiler_params=_params("parallel"),
        name="prep",
    )(x, g.reshape(1, d))


def _final_norm_kernel(x_ref, g_ref, o_ref):
    x = x_ref[...]
    y = x * lax.rsqrt(jnp.mean(x * x, axis=-1, keepdims=True) + EPS)
    o_ref[...] = y * g_ref[...]


def _final_norm(x, g):
    m, d = x.shape
    bm = _tile(m, ROW_TILE)
    return pl.pallas_call(
        _final_norm_kernel,
        grid=(m // bm,),
        in_specs=[pl.BlockSpec((bm, d), lambda i: (i, 0)),
                  pl.BlockSpec((1, d), lambda i: (0, 0))],
        out_specs=pl.BlockSpec((bm, d), lambda i: (i, 0)),
        out_shape=jax.ShapeDtypeStruct((m, d), F32),
        compiler_params=_params("parallel"),
        name="final_norm",
    )(x, g.reshape(1, d))


def _mm_kernel(*refs, nk, has_scale, has_res, act, n_groups, post_scale):
    refs = list(refs)
    a_ref, w_ref = refs[0], refs[1]
    pos = 2
    scale_ref = res_ref = None
    if has_scale:
        scale_ref = refs[pos]
        pos += 1
    if has_res:
        res_ref = refs[pos]
        pos += 1
    o_ref = refs[pos]
    acc_ref = refs[pos + 1] if nk > 1 else None

    def epilogue(acc):
        if has_scale:
            acc = acc * scale_ref[...]
        if post_scale is not None:
            acc = acc * post_scale
        if act == "relu2":
            acc = jnp.square(jnp.maximum(acc, 0.0))
        elif act == "group_softmax":
            gw = acc.shape[1] // n_groups
            parts = []
            for gi in range(n_groups):
                s = acc[:, gi * gw:(gi + 1) * gw]
                e = jnp.exp(s - jnp.max(s, axis=-1, keepdims=True))
                parts.append(e / jnp.sum(e, axis=-1, keepdims=True))
            acc = jnp.concatenate(parts, axis=1)
        if has_res:
            acc = acc + res_ref[...]
        o_ref[...] = acc.astype(o_ref.dtype)

    prod = jnp.dot(a_ref[...], w_ref[...], preferred_element_type=F32)
    if nk == 1:
        epilogue(prod)
    else:
        k = pl.program_id(2)

        @pl.when(k == 0)
        def _():
            acc_ref[...] = prod

        @pl.when(jnp.logical_and(k > 0, k < nk - 1))
        def _():
            acc_ref[...] += prod

        @pl.when(k == nk - 1)
        def _():
            epilogue(acc_ref[...] + prod)


def _mm(a, w, *, scale=None, post_scale=None, res=None, act=None, n_groups=1, out_dtype=BF16,
        bm=None, bn=None, bk=None, name="mm"):
    m, kdim = a.shape
    if w.ndim == 2:
        w = w[None]
    nb, _, n = w.shape
    rows_per_w = m // nb
    bm = _tile(rows_per_w, bm or MM_BM)
    bn = _tile(n, bn or MM_BN)
    bk = _tile(kdim, bk or MM_BK)
    nk = kdim // bk
    blocks_per_w = rows_per_w // bm

    in_specs = [pl.BlockSpec((bm, bk), lambda i, j, k: (i, k)),
                pl.BlockSpec((None, bk, bn), lambda i, j, k: (i // blocks_per_w, k, j))]
    args = [a, w]
    if scale is not None:
        in_specs.append(pl.BlockSpec((bm, 1), lambda i, j, k: (i, 0)))
        args.append(scale)
    if res is not None:
        in_specs.append(pl.BlockSpec((bm, bn), lambda i, j, k: (i, j)))
        args.append(res)
    kern = functools.partial(_mm_kernel, nk=nk, has_scale=scale is not None,
                             has_res=res is not None, act=act, n_groups=n_groups,
                             post_scale=post_scale)
    return pl.pallas_call(
        kern,
        grid=(m // bm, n // bn, nk),
        in_specs=in_specs,
        out_specs=pl.BlockSpec((bm, bn), lambda i, j, k: (i, j)),
        out_shape=jax.ShapeDtypeStruct((m, n), out_dtype),
        scratch_shapes=[pltpu.VMEM((bm, bn), F32)] if nk > 1 else [],
        compiler_params=_params("parallel", "parallel", "arbitrary"),
        name=name,
    )(*args)


def _mix_kernel(z_ref, halo_ref, pw_ref, ps_ref, lg_ref, lb_ref, sw_ref, sb_ref,
                y_ref, v_ref, *, seq, wpool, wsgu):
    t = z_ref.shape[0]
    n_groups = len(POOL_WINDOWS)
    gc = wpool // n_groups
    pos0 = (pl.program_id(0) * t) % seq

    halo = halo_ref[...]
    halo = jnp.where(pos0 == 0, jnp.zeros_like(halo), halo)
    row = lax.broadcasted_iota(jnp.int32, (t, t + POOL_HALO), 0)
    col = lax.broadcasted_iota(jnp.int32, (t, t + POOL_HALO), 1)
    lag = row + POOL_HALO - col
    pos = pos0 + lax.broadcasted_iota(jnp.int32, (t, 1), 0)
    for gi, win in enumerate(POOL_WINDOWS):
        cs = slice(gi * gc, (gi + 1) * gc)
        zg = z_ref[:, cs]
        band = jnp.logical_and(lag >= 0, lag < win).astype(BF16)
        wsum = jnp.dot(band, jnp.concatenate([halo[:, cs], zg], axis=0),
                       preferred_element_type=F32)
        cnt = jnp.minimum(pos + 1, win).astype(F32)
        d = wsum / cnt - zg.astype(F32)
        yg = jnp.dot(d.astype(BF16), pw_ref[gi], preferred_element_type=F32)
        y_ref[:, cs] = (yg * ps_ref[:, cs]).astype(BF16)

    gv = jax.nn.gelu(z_ref[:, wpool + wsgu:].astype(F32))
    mu = jnp.mean(gv, axis=-1, keepdims=True)
    cen = gv - mu
    var = jnp.mean(cen * cen, axis=-1, keepdims=True)
    v_ref[...] = (cen * lax.rsqrt(var + EPS) * lg_ref[...] + lb_ref[...]).astype(BF16)

    tri = (lax.broadcasted_iota(jnp.int32, (CHUNK, CHUNK), 0)
           >= lax.broadcasted_iota(jnp.int32, (CHUNK, CHUNK), 1))
    for h in range(wsgu // SGU_HEAD):
        hs = slice(h * SGU_HEAD, (h + 1) * SGU_HEAD)
        us = slice(wpool + h * SGU_HEAD, wpool + (h + 1) * SGU_HEAD)
        wm = jnp.where(tri, sw_ref[h], jnp.zeros((CHUNK, CHUNK), BF16))
        for c in range(t // CHUNK):
            rs = slice(c * CHUNK, (c + 1) * CHUNK)
            mixed = jnp.dot(wm, v_ref[rs, hs], preferred_element_type=F32) + sb_ref[:, hs]
            u = jax.nn.gelu(z_ref[rs, us].astype(F32))
            y_ref[rs, us] = (u * mixed).astype(BF16)


def _mix(z, pool_w, pool_scale, ln_g, ln_b, sgu_w, sgu_b, *, seq):
    m, d_in = z.shape
    n_groups, gc, _ = pool_w.shape
    wpool = n_groups * gc
    wsgu = (d_in - wpool) // 2
    n_heads = wsgu // SGU_HEAD
    assert n_groups == len(POOL_WINDOWS) and sgu_w.shape == (n_heads, CHUNK, CHUNK)
    t = _tile(seq, ROW_TILE)
    assert t % CHUNK == 0 and t % POOL_HALO == 0
    halo_blocks = t // POOL_HALO
    sb = jnp.repeat(jnp.transpose(sgu_b), SGU_HEAD, axis=1)
    kern = functools.partial(_mix_kernel, seq=seq, wpool=wpool, wsgu=wsgu)
    const2 = lambda i: (0, 0)
    return pl.pallas_call(
        kern,
        grid=(m // t,),
        in_specs=[pl.BlockSpec((t, d_in), lambda i: (i, 0)),
                  pl.BlockSpec((POOL_HALO, wpool),
                               lambda i: (jnp.maximum(i * halo_blocks - 1, 0), 0)),
                  pl.BlockSpec((n_groups, gc, gc), lambda i: (0, 0, 0)),
                  pl.BlockSpec((1, wpool), const2),
                  pl.BlockSpec((1, wsgu), const2),
                  pl.BlockSpec((1, wsgu), const2),
                  pl.BlockSpec((n_heads, CHUNK, CHUNK), lambda i: (0, 0, 0)),
                  pl.BlockSpec((CHUNK, wsgu), const2)],
        out_specs=pl.BlockSpec((t, wpool + wsgu), lambda i: (i, 0)),
        out_shape=jax.ShapeDtypeStruct((m, wpool + wsgu), BF16),
        scratch_shapes=[pltpu.VMEM((t, wsgu), BF16)],
        compiler_params=_params("parallel"),
        name="mix",
    )(z, z, pool_w, pool_scale.reshape(1, wpool), ln_g.reshape(1, wsgu),
      ln_b.reshape(1, wsgu), sgu_w, sb)


def _fold_qk_kernel(wq_ref, k_ref, a_ref):
    a_ref[...] = lax.dot_general(wq_ref[...], k_ref[...], (((1,), (1,)), ((), ())),
                                 preferred_element_type=F32).astype(a_ref.dtype)


def _fold_vo_kernel(v_ref, wo_ref, b_ref):
    b_ref[...] = jnp.dot(v_ref[...], wo_ref[...],
                         preferred_element_type=F32).astype(b_ref.dtype)


def _fold_qk(wq, k, *, n_batch, n_mem):
    d = wq.shape[0]
    dh = d // N_XHEADS
    return pl.pallas_call(
        _fold_qk_kernel,
        grid=(N_XHEADS, n_batch),
        in_specs=[pl.BlockSpec((d, dh), lambda h, b: (0, h)),
                  pl.BlockSpec((n_mem, dh), lambda h, b: (b, h))],
        out_specs=pl.BlockSpec((None, d, n_mem), lambda h, b: (b, 0, h)),
        out_shape=jax.ShapeDtypeStruct((n_batch, d, N_XHEADS * n_mem), BF16),
        compiler_params=_params("parallel", "parallel"),
        name="fold_qk",
    )(wq, k)


def _fold_vo(v, wo, *, n_batch, n_mem):
    d = wo.shape[1]
    dh = d // N_XHEADS
    return pl.pallas_call(
        _fold_vo_kernel,
        grid=(N_XHEADS, n_batch),
        in_specs=[pl.BlockSpec((n_mem, dh), lambda h, b: (b, h)),
                  pl.BlockSpec((dh, d), lambda h, b: (h, 0))],
        out_specs=pl.BlockSpec((None, n_mem, d), lambda h, b: (b, h, 0)),
        out_shape=jax.ShapeDtypeStruct((n_batch, N_XHEADS * n_mem, d), BF16),
        compiler_params=_params("parallel", "parallel"),
        name="fold_vo",
    )(v, wo)


def kernel(x, mem, ln_mix, w_in, pool_w, pool_scale, sgu_ln_g, sgu_ln_b, sgu_w, sgu_b,
           w_out, ln_x, ln_mem, w_q, w_k, w_v, w_o, ln_ffn, w_up, w_down, ln_final):
    n_batch, seq, d = x.shape
    n_mem = mem.shape[1]
    depth = w_in.shape[0]
    dh = d // N_XHEADS
    x = x.reshape(n_batch * seq, d)
    memf = mem.reshape(n_batch * n_mem, d)

    bf = lambda w: w.astype(BF16)
    w_in, pool_w, sgu_w, w_out = bf(w_in), bf(pool_w), bf(sgu_w), bf(w_out)
    w_q, w_k, w_v, w_o, w_up, w_down = bf(w_q), bf(w_k), bf(w_v), bf(w_o), bf(w_up), bf(w_down)

    for l in range(depth):
        xb, r = _prep(x, ln_mix[l])
        z = _mm(xb, w_in[l], scale=r, name="mm_in")
        y = _mix(z, pool_w[l], pool_scale[l], sgu_ln_g[l], sgu_ln_b[l], sgu_w[l], sgu_b[l],
                 seq=seq)
        x = _mm(y, w_out[l], res=x, out_dtype=F32, name="mm_out")

        mb, rm = _prep(memf, ln_mem[l])
        k = _mm(mb, w_k[l], scale=rm, name="mm_k")
        v = _mm(mb, w_v[l], scale=rm, name="mm_v")
        a_fold = _fold_qk(w_q[l], k, n_batch=n_batch, n_mem=n_mem)
        b_fold = _fold_vo(v, w_o[l], n_batch=n_batch, n_mem=n_mem)
        xb, r = _prep(x, ln_x[l])
        p = _mm(xb, a_fold, scale=r, post_scale=dh ** -0.5, act="group_softmax",
                n_groups=N_XHEADS,
                bn=N_XHEADS * n_mem, name="mm_scores")
        x = _mm(p, b_fold, res=x, out_dtype=F32, name="mm_attn_out")

        xb, r = _prep(x, ln_ffn[l])
        hid = _mm(xb, w_up[l], scale=r, act="relu2", name="mm_up")
        x = _mm(hid, w_down[l], res=x, out_dtype=F32, name="mm_down")

    return _final_norm(x, ln_final).reshape(n_batch, seq, d)
```

```python
import functools

import jax
import jax.numpy as jnp
from jax import lax
from jax.experimental import pallas as pl
from jax.experimental.pallas import tpu as pltpu

EPS = 1e-6
POOL_WINDOWS = (2, 4, 8, 16)
POOL_HALO = 16
CHUNK = 128
SGU_HEAD = 128
N_XHEADS = 4
LANES = 128

V7X_VMEM_LIMIT_BYTES = 56 * 1024 * 1024

MM_BM = 1024
MM_BN = 1024
MM_BN_RES = 512
MM_BK = 4096
MM_BK_DOWN = 2048
ROW_TILE = 256
FOLD_TILE = 1024

F32 = jnp.float32
BF16 = jnp.bfloat16


def _tile(dim, pref):
    t = min(dim, pref)
    assert dim % t == 0, (dim, pref)
    return t


def _params(*sem):
    return pltpu.CompilerParams(dimension_semantics=sem,
                                vmem_limit_bytes=V7X_VMEM_LIMIT_BYTES)


def _fold_lanes(v):
    acc = v[:, :LANES]
    for c in range(1, v.shape[1] // LANES):
        acc = acc + v[:, c * LANES:(c + 1) * LANES]
    return acc


def _prep_kernel(x_ref, g_ref, xb_ref, ssq_ref):
    x = x_ref[...]
    ssq_ref[...] = _fold_lanes(x * x)
    xb_ref[...] = (x * g_ref[...]).astype(BF16)


def _prep(x, g):
    m, d = x.shape
    bm = _tile(m, ROW_TILE)
    return pl.pallas_call(
        _prep_kernel,
        grid=(m // bm,),
        in_specs=[pl.BlockSpec((bm, d), lambda i: (i, 0)),
                  pl.BlockSpec((1, d), lambda i: (0, 0))],
        out_specs=[pl.BlockSpec((bm, d), lambda i: (i, 0)),
                   pl.BlockSpec((bm, LANES), lambda i: (i, 0))],
        out_shape=[jax.ShapeDtypeStruct((m, d), BF16),
                   jax.ShapeDtypeStruct((m, LANES), F32)],
        compiler_params=_params("parallel"),
        name="prep",
    )(x, g.reshape(1, d))


def _final_norm_kernel(x_ref, g_ref, o_ref):
    x = x_ref[...]
    y = x * lax.rsqrt(jnp.mean(x * x, axis=-1, keepdims=True) + EPS)
    o_ref[...] = y * g_ref[...]


def _final_norm(x, g):
    m, d = x.shape
    bm = _tile(m, ROW_TILE)
    return pl.pallas_call(
        _final_norm_kernel,
        grid=(m // bm,),
        in_specs=[pl.BlockSpec((bm, d), lambda i: (i, 0)),
                  pl.BlockSpec((1, d), lambda i: (0, 0))],
        out_specs=pl.BlockSpec((bm, d), lambda i: (i, 0)),
        out_shape=jax.ShapeDtypeStruct((m, d), F32),
        compiler_params=_params("parallel"),
        name="final_norm",
    )(x, g.reshape(1, d))


def _mm_kernel(*refs, nk, has_ssq, has_res, has_gain, cast_w, act, n_groups, post_scale,
               inv_k):
    refs = list(refs)
    a_ref, w_ref = refs[0], refs[1]
    pos = 2
    ssq_ref = res_ref = gain_ref = None
    if has_ssq:
        ssq_ref = refs[pos]
        pos += 1
    if has_res:
        res_ref = refs[pos]
        pos += 1
    if has_gain:
        gain_ref = refs[pos]
        pos += 1
    o_ref = refs[pos]
    xb_ref, ssq_out_ref = (refs[pos + 1], refs[pos + 2]) if has_gain else (None, None)

    def emit(new):
        xb_ref[...] = (new * gain_ref[...]).astype(BF16)
        ssq_out_ref[...] = _fold_lanes(new * new)

    w = w_ref[...]
    if cast_w:
        w = w.astype(BF16)
    prod = jnp.dot(a_ref[...], w, preferred_element_type=F32)

    if nk > 1:
        assert has_res and not has_ssq and act is None and post_scale is None

        @pl.when(pl.program_id(2) == 0)
        def _():
            o_ref[...] = res_ref[...]

        new = o_ref[...] + prod
        o_ref[...] = new
        if has_gain:
            emit(new)
        return

    acc = prod
    if has_ssq:
        acc = acc * lax.rsqrt(jnp.sum(ssq_ref[...], axis=-1, keepdims=True) * inv_k + EPS)
    if post_scale is not None:
        acc = acc * post_scale
    if act == "relu2":
        acc = jnp.square(jnp.maximum(acc, 0.0))
    elif act == "group_softmax":
        gw = acc.shape[1] // n_groups
        parts = []
        for gi in range(n_groups):
            s = acc[:, gi * gw:(gi + 1) * gw]
            e = jnp.exp(s - jnp.max(s, axis=-1, keepdims=True))
            parts.append(e / jnp.sum(e, axis=-1, keepdims=True))
        acc = jnp.concatenate(parts, axis=1)
    if has_res:
        acc = acc + res_ref[...]
    o_ref[...] = acc.astype(o_ref.dtype)
    if has_gain:
        emit(acc)


def _mm(a, w, *, layer=None, ssq=None, post_scale=None, res=None, act=None, n_groups=1,
        gain=None, out_dtype=BF16, bm=None, bn=None, bk=None, name="mm"):
    m, kdim = a.shape
    nl, _, n = w.shape
    rows_per_w = m if layer is not None else m // nl
    bm = _tile(rows_per_w, bm or MM_BM)
    bn = _tile(n, bn or MM_BN)
    bk = _tile(kdim, bk or MM_BK)
    nk = kdim // bk
    blocks_per_w = rows_per_w // bm
    if layer is not None:
        w_map = lambda i, j, k: (layer, k, j)
    else:
        w_map = lambda i, j, k: (i // blocks_per_w, k, j)

    in_specs = [pl.BlockSpec((bm, bk), lambda i, j, k: (i, k)),
                pl.BlockSpec((None, bk, bn), w_map)]
    args = [a, w]
    if ssq is not None:
        in_specs.append(pl.BlockSpec((bm, ssq.shape[1]), lambda i, j, k: (i, 0)))
        args.append(ssq)
    if res is not None:
        in_specs.append(pl.BlockSpec((bm, bn), lambda i, j, k: (i, j)))
        args.append(res)
    out_specs = [pl.BlockSpec((bm, bn), lambda i, j, k: (i, j))]
    out_shape = [jax.ShapeDtypeStruct((m, n), out_dtype)]
    if gain is not None:
        in_specs.append(pl.BlockSpec((1, bn), lambda i, j, k: (0, j)))
        args.append(gain.reshape(1, n))
        out_specs += [pl.BlockSpec((bm, bn), lambda i, j, k: (i, j)),
                      pl.BlockSpec((bm, LANES), lambda i, j, k: (i, j))]
        out_shape += [jax.ShapeDtypeStruct((m, n), BF16),
                      jax.ShapeDtypeStruct((m, LANES * (n // bn)), F32)]
    kern = functools.partial(
        _mm_kernel, nk=nk, has_ssq=ssq is not None, has_res=res is not None,
        has_gain=gain is not None, cast_w=w.dtype != BF16, act=act, n_groups=n_groups,
        post_scale=post_scale, inv_k=1.0 / kdim)
    outs = pl.pallas_call(
        kern,
        grid=(m // bm, n // bn, nk),
        in_specs=in_specs,
        out_specs=out_specs,
        out_shape=out_shape,
        compiler_params=_params("parallel", "parallel", "arbitrary"),
        name=name,
    )(*args)
    return outs if gain is not None else outs[0]


def _mix_kernel(z_ref, halo_ref, pw_ref, ps_ref, lg_ref, lb_ref, sw_ref, sb_ref,
                y_ref, v_ref, *, seq, wpool, wsgu):
    t = z_ref.shape[0]
    n_groups = len(POOL_WINDOWS)
    gc = wpool // n_groups
    pos0 = (pl.program_id(0) * t) % seq

    halo = halo_ref[...]
    halo = jnp.where(pos0 == 0, jnp.zeros_like(halo), halo)
    row = lax.broadcasted_iota(jnp.int32, (t, t + POOL_HALO), 0)
    col = lax.broadcasted_iota(jnp.int32, (t, t + POOL_HALO), 1)
    lag = row + POOL_HALO - col
    pos = pos0 + lax.broadcasted_iota(jnp.int32, (t, 1), 0)
    for gi, win in enumerate(POOL_WINDOWS):
        cs = slice(gi * gc, (gi + 1) * gc)
        zg = z_ref[:, cs]
        band = jnp.logical_and(lag >= 0, lag < win).astype(BF16)
        wsum = jnp.dot(band, jnp.concatenate([halo[:, cs], zg], axis=0),
                       preferred_element_type=F32)
        cnt = jnp.minimum(pos + 1, win).astype(F32)
        d = wsum / cnt - zg.astype(F32)
        yg = jnp.dot(d.astype(BF16), pw_ref[gi], preferred_element_type=F32)
        y_ref[:, cs] = (yg * ps_ref[:, cs]).astype(BF16)

    gv = jax.nn.gelu(z_ref[:, wpool + wsgu:].astype(F32))
    mu = jnp.mean(gv, axis=-1, keepdims=True)
    cen = gv - mu
    var = jnp.mean(cen * cen, axis=-1, keepdims=True)
    v_ref[...] = (cen * lax.rsqrt(var + EPS) * lg_ref[...] + lb_ref[...]).astype(BF16)

    tri = (lax.broadcasted_iota(jnp.int32, (CHUNK, CHUNK), 0)
           >= lax.broadcasted_iota(jnp.int32, (CHUNK, CHUNK), 1))
    for h in range(wsgu // SGU_HEAD):
        hs = slice(h * SGU_HEAD, (h + 1) * SGU_HEAD)
        us = slice(wpool + h * SGU_HEAD, wpool + (h + 1) * SGU_HEAD)
        wm = jnp.where(tri, sw_ref[h], jnp.zeros((CHUNK, CHUNK), BF16))
        for c in range(t // CHUNK):
            rs = slice(c * CHUNK, (c + 1) * CHUNK)
            mixed = jnp.dot(wm, v_ref[rs, hs], preferred_element_type=F32) + sb_ref[:, hs]
            u = jax.nn.gelu(z_ref[rs, us].astype(F32))
            y_ref[rs, us] = (u * mixed).astype(BF16)


def _mix(z, pool_w, pool_scale, ln_g, ln_b, sgu_w, sgu_b, *, layer, seq):
    m, d_in = z.shape
    _, n_groups, gc, _ = pool_w.shape
    wpool = n_groups * gc
    wsgu = (d_in - wpool) // 2
    n_heads = wsgu // SGU_HEAD
    assert n_groups == len(POOL_WINDOWS) and sgu_w.shape[1:] == (n_heads, CHUNK, CHUNK)
    t = _tile(seq, ROW_TILE)
    assert t % CHUNK == 0 and t % POOL_HALO == 0
    halo_blocks = t // POOL_HALO
    sb = jnp.repeat(jnp.transpose(sgu_b), SGU_HEAD, axis=1)
    kern = functools.partial(_mix_kernel, seq=seq, wpool=wpool, wsgu=wsgu)
    const2 = lambda i: (0, 0)
    return pl.pallas_call(
        kern,
        grid=(m // t,),
        in_specs=[pl.BlockSpec((t, d_in), lambda i: (i, 0)),
                  pl.BlockSpec((POOL_HALO, wpool),
                               lambda i: (jnp.maximum(i * halo_blocks - 1, 0), 0)),
                  pl.BlockSpec((None, n_groups, gc, gc), lambda i: (layer, 0, 0, 0)),
                  pl.BlockSpec((1, wpool), const2),
                  pl.BlockSpec((1, wsgu), const2),
                  pl.BlockSpec((1, wsgu), const2),
                  pl.BlockSpec((None, n_heads, CHUNK, CHUNK), lambda i: (layer, 0, 0, 0)),
                  pl.BlockSpec((CHUNK, wsgu), const2)],
        out_specs=pl.BlockSpec((t, wpool + wsgu), lambda i: (i, 0)),
        out_shape=jax.ShapeDtypeStruct((m, wpool + wsgu), BF16),
        scratch_shapes=[pltpu.VMEM((t, wsgu), BF16)],
        compiler_params=_params("parallel"),
        name="mix",
    )(z, z, pool_w, pool_scale.reshape(1, wpool), ln_g.reshape(1, wsgu),
      ln_b.reshape(1, wsgu), sgu_w, sb)


def _fold_qk_kernel(wq_ref, k_ref, a_ref):
    n_mem = a_ref.shape[2]
    s = lax.dot_general(wq_ref[...].astype(BF16), k_ref[...], (((1,), (1,)), ((), ())),
                        preferred_element_type=F32)
    for b in range(a_ref.shape[0]):
        a_ref[b] = s[:, b * n_mem:(b + 1) * n_mem].astype(a_ref.dtype)


def _fold_vo_kernel(v_ref, wo_ref, b_ref):
    n_mem = b_ref.shape[1]
    s = jnp.dot(v_ref[...], wo_ref[...].astype(BF16), preferred_element_type=F32)
    for b in range(b_ref.shape[0]):
        b_ref[b] = s[b * n_mem:(b + 1) * n_mem].astype(b_ref.dtype)


def _fold_qk(wq, k, *, layer, n_batch, n_mem):
    d = wq.shape[1]
    dh = d // N_XHEADS
    rb = _tile(d, FOLD_TILE)
    return pl.pallas_call(
        _fold_qk_kernel,
        grid=(N_XHEADS, d // rb),
        in_specs=[pl.BlockSpec((None, rb, dh), lambda h, r: (layer, r, h)),
                  pl.BlockSpec((n_batch * n_mem, dh), lambda h, r: (0, h))],
        out_specs=pl.BlockSpec((n_batch, rb, n_mem), lambda h, r: (0, r, h)),
        out_shape=jax.ShapeDtypeStruct((n_batch, d, N_XHEADS * n_mem), BF16),
        compiler_params=_params("parallel", "parallel"),
        name="fold_qk",
    )(wq, k)


def _fold_vo(v, wo, *, layer, n_batch, n_mem):
    d = wo.shape[2]
    dh = d // N_XHEADS
    cb = _tile(d, FOLD_TILE)
    return pl.pallas_call(
        _fold_vo_kernel,
        grid=(N_XHEADS, d // cb),
        in_specs=[pl.BlockSpec((n_batch * n_mem, dh), lambda h, c: (0, h)),
                  pl.BlockSpec((None, dh, cb), lambda h, c: (layer, h, c))],
        out_specs=pl.BlockSpec((n_batch, n_mem, cb), lambda h, c: (0, h, c)),
        out_shape=jax.ShapeDtypeStruct((n_batch, N_XHEADS * n_mem, d), BF16),
        compiler_params=_params("parallel", "parallel"),
        name="fold_vo",
    )(v, wo)


def kernel(x, mem, ln_mix, w_in, pool_w, pool_scale, sgu_ln_g, sgu_ln_b, sgu_w, sgu_b,
           w_out, ln_x, ln_mem, w_q, w_k, w_v, w_o, ln_ffn, w_up, w_down, ln_final):
    n_batch, seq, d = x.shape
    n_mem = mem.shape[1]
    depth = w_in.shape[0]
    dh = d // N_XHEADS
    x = x.reshape(n_batch * seq, d)
    memf = mem.reshape(n_batch * n_mem, d)

    w_in, w_out, w_up, w_down = (w.astype(BF16) for w in (w_in, w_out, w_up, w_down))
    pool_w, sgu_w = pool_w.astype(BF16), sgu_w.astype(BF16)

    xb, ssq = _prep(x, ln_mix[0])
    for l in range(depth):
        z = _mm(xb, w_in, layer=l, ssq=ssq, name="mm_in")
        y = _mix(z, pool_w, pool_scale[l], sgu_ln_g[l], sgu_ln_b[l], sgu_w, sgu_b[l],
                 layer=l, seq=seq)
        x, xb, ssq = _mm(y, w_out, layer=l, res=x, gain=ln_x[l], out_dtype=F32,
                         bn=MM_BN_RES, name="mm_out")

        mb, mssq = _prep(memf, ln_mem[l])
        k = _mm(mb, w_k, layer=l, ssq=mssq, bn=MM_BN_RES, name="mm_k")
        v = _mm(mb, w_v, layer=l, ssq=mssq, bn=MM_BN_RES, name="mm_v")
        a_fold = _fold_qk(w_q, k, layer=l, n_batch=n_batch, n_mem=n_mem)
        b_fold = _fold_vo(v, w_o, layer=l, n_batch=n_batch, n_mem=n_mem)
        p = _mm(xb, a_fold, ssq=ssq, post_scale=dh ** -0.5, act="group_softmax",
                n_groups=N_XHEADS, bn=N_XHEADS * n_mem, name="mm_scores")
        x, xb, ssq = _mm(p, b_fold, res=x, gain=ln_ffn[l], out_dtype=F32, name="mm_attn_out")

        hid = _mm(xb, w_up, layer=l, ssq=ssq, act="relu2", name="mm_up")
        if l + 1 < depth:
            x, xb, ssq = _mm(hid, w_down, layer=l, res=x, gain=ln_mix[l + 1], out_dtype=F32,
                             bk=MM_BK_DOWN, name="mm_down")
        else:
            x = _mm(hid, w_down, layer=l, res=x, out_dtype=F32, bk=MM_BK_DOWN, name="mm_down")

    return _final_norm(x, ln_final).reshape(n_batch, seq, d)
```

```python
import functools

import jax
import jax.numpy as jnp
from jax import lax
from jax.experimental import pallas as pl
from jax.experimental.pallas import tpu as pltpu

EPS = 1e-6
POOL_WINDOWS = (2, 4, 8, 16)
POOL_HALO = 16
CHUNK = 128
SGU_HEAD = 128
N_XHEADS = 4
LANES = 128
BF16_SUBLANES = 16

V7X_VMEM_LIMIT_BYTES = 56 * 1024 * 1024

MM_BM = 1024
MM_BN = 1024
MM_BN_RES = 512
MM_BK = 4096
MM_BK_DOWN = 2048
ROW_TILE = 256
FOLD_TILE = 1024

F32 = jnp.float32
BF16 = jnp.bfloat16


def _tile(dim, pref):
    t = min(dim, pref)
    assert dim % t == 0, (dim, pref)
    return t


def _params(*sem):
    return pltpu.CompilerParams(dimension_semantics=sem,
                                vmem_limit_bytes=V7X_VMEM_LIMIT_BYTES)


def _fold_lanes(v):
    acc = v[:, :LANES]
    for c in range(1, v.shape[1] // LANES):
        acc = acc + v[:, c * LANES:(c + 1) * LANES]
    return acc


def _prep_kernel(x_ref, g_ref, xb_ref, ssq_ref):
    x = x_ref[...]
    ssq_ref[...] = _fold_lanes(x * x)
    xb_ref[...] = (x * g_ref[...]).astype(BF16)


def _prep(x, g):
    m, d = x.shape
    bm = _tile(m, ROW_TILE)
    return pl.pallas_call(
        _prep_kernel,
        grid=(m // bm,),
        in_specs=[pl.BlockSpec((bm, d), lambda i: (i, 0)),
                  pl.BlockSpec((1, d), lambda i: (0, 0))],
        out_specs=[pl.BlockSpec((bm, d), lambda i: (i, 0)),
                   pl.BlockSpec((bm, LANES), lambda i: (i, 0))],
        out_shape=[jax.ShapeDtypeStruct((m, d), BF16),
                   jax.ShapeDtypeStruct((m, LANES), F32)],
        compiler_params=_params("parallel"),
        name="prep",
    )(x, g.reshape(1, d))


def _final_norm_kernel(x_ref, g_ref, o_ref):
    x = x_ref[...]
    y = x * lax.rsqrt(jnp.mean(x * x, axis=-1, keepdims=True) + EPS)
    o_ref[...] = y * g_ref[...]


def _final_norm(x, g):
    m, d = x.shape
    bm = _tile(m, ROW_TILE)
    return pl.pallas_call(
        _final_norm_kernel,
        grid=(m // bm,),
        in_specs=[pl.BlockSpec((bm, d), lambda i: (i, 0)),
                  pl.BlockSpec((1, d), lambda i: (0, 0))],
        out_specs=pl.BlockSpec((bm, d), lambda i: (i, 0)),
        out_shape=jax.ShapeDtypeStruct((m, d), F32),
        compiler_params=_params("parallel"),
        name="final_norm",
    )(x, g.reshape(1, d))


def _mm_kernel(*refs, nk, has_ssq, has_res, has_gain, n_casts, cast_w, act, n_groups,
               post_scale, inv_k):
    refs = list(refs)
    a_ref, w_ref = refs[0], refs[1]
    pos = 2
    ssq_ref = res_ref = gain_ref = None
    if has_ssq:
        ssq_ref = refs[pos]
        pos += 1
    if has_res:
        res_ref = refs[pos]
        pos += 1
    if has_gain:
        gain_ref = refs[pos]
        pos += 1
    cast_in = refs[pos:pos + n_casts]
    pos += n_casts
    o_ref = refs[pos]
    pos += 1
    xb_ref = ssq_out_ref = None
    if has_gain:
        xb_ref, ssq_out_ref = refs[pos], refs[pos + 1]
        pos += 2
    cast_out = refs[pos:pos + n_casts]
    pos += n_casts
    acc_ref = refs[pos] if nk > 1 else None

    for ci, co in zip(cast_in, cast_out):
        co[...] = ci[...].astype(BF16)

    def emit(new):
        xb_ref[...] = (new * gain_ref[...]).astype(BF16)
        ssq_out_ref[...] = _fold_lanes(new * new)

    def dot():
        w = w_ref[...]
        if cast_w:
            w = w.astype(BF16)
        return jnp.dot(a_ref[...], w, preferred_element_type=F32)

    if nk > 1:
        assert nk % 2 == 0 and has_res and not has_ssq and act is None and post_scale is None
        k = pl.program_id(2)

        @pl.when(k == 0)
        def _():
            acc_ref[...] = res_ref[...] + dot()

        @pl.when(jnp.logical_and(k > 0, k % 2 == 0))
        def _():
            acc_ref[...] = o_ref[...] + dot()

        @pl.when(k % 2 == 1)
        def _():
            new = acc_ref[...] + dot()
            o_ref[...] = new
            if has_gain:
                emit(new)
        return

    acc = dot()
    if has_ssq:
        acc = acc * lax.rsqrt(jnp.sum(ssq_ref[...], axis=-1, keepdims=True) * inv_k + EPS)
    if post_scale is not None:
        acc = acc * post_scale
    if act == "relu2":
        acc = jnp.square(jnp.maximum(acc, 0.0))
    elif act == "group_softmax":
        gw = acc.shape[1] // n_groups
        parts = []
        for gi in range(n_groups):
            s = acc[:, gi * gw:(gi + 1) * gw]
            e = jnp.exp(s - jnp.max(s, axis=-1, keepdims=True))
            parts.append(e / jnp.sum(e, axis=-1, keepdims=True))
        acc = jnp.concatenate(parts, axis=1)
    if has_res:
        acc = acc + res_ref[...]
    o_ref[...] = acc.astype(o_ref.dtype)
    if has_gain:
        emit(acc)


def _mm(a, w, *, layer=None, ssq=None, post_scale=None, res=None, act=None, n_groups=1,
        gain=None, casts=(), out_dtype=BF16, bm=None, bn=None, bk=None, name="mm"):
    m, kdim = a.shape
    nl, _, n = w.shape
    rows_per_w = m if layer is not None else m // nl
    bm = _tile(rows_per_w, bm or MM_BM)
    bn = _tile(n, bn or MM_BN)
    bk = _tile(kdim, bk or MM_BK)
    nk = kdim // bk
    blocks_per_w = rows_per_w // bm
    if layer is not None:
        w_map = lambda i, j, k: (layer, k, j)
    else:
        w_map = lambda i, j, k: (i // blocks_per_w, k, j)

    in_specs = [pl.BlockSpec((bm, bk), lambda i, j, k: (i, k)),
                pl.BlockSpec((None, bk, bn), w_map)]
    args = [a, w]
    if ssq is not None:
        in_specs.append(pl.BlockSpec((bm, ssq.shape[1]), lambda i, j, k: (i, 0)))
        args.append(ssq)
    if res is not None:
        in_specs.append(pl.BlockSpec((bm, bn), lambda i, j, k: (i, j)))
        args.append(res)
    out_specs = [pl.BlockSpec((bm, bn), lambda i, j, k: (i, j))]
    out_shape = [jax.ShapeDtypeStruct((m, n), out_dtype)]
    if gain is not None:
        in_specs.append(pl.BlockSpec((1, bn), lambda i, j, k: (0, j)))
        args.append(gain.reshape(1, n))
        out_specs += [pl.BlockSpec((bm, bn), lambda i, j, k: (i, j)),
                      pl.BlockSpec((bm, LANES), lambda i, j, k: (i, j))]
        out_shape += [jax.ShapeDtypeStruct((m, n), BF16),
                      jax.ShapeDtypeStruct((m, LANES * (n // bn)), F32)]
    n_main = len(out_shape)
    gi, gj = m // bm, n // bn
    for src, src_layer in casts:
        _, kw, nw = src.shape
        cj = max(c for c in range(1, gj + 1)
                 if kw % (gi * c) == 0 and (kw // (gi * c)) % BF16_SUBLANES == 0)
        rows = kw // (gi * cj)
        slab = lambda i, j, k, cj=cj: i * cj + jnp.minimum(j, cj - 1)
        in_specs.append(pl.BlockSpec(
            (None, rows, nw), lambda i, j, k, sl=slab, l=src_layer: (l, sl(i, j, k), 0)))
        args.append(src)
        out_specs.append(pl.BlockSpec(
            (None, rows, nw), lambda i, j, k, sl=slab: (0, sl(i, j, k), 0)))
        out_shape.append(jax.ShapeDtypeStruct((1, kw, nw), BF16))
    kern = functools.partial(
        _mm_kernel, nk=nk, has_ssq=ssq is not None, has_res=res is not None,
        has_gain=gain is not None, n_casts=len(casts), cast_w=w.dtype != BF16, act=act,
        n_groups=n_groups, post_scale=post_scale, inv_k=1.0 / kdim)
    outs = pl.pallas_call(
        kern,
        grid=(gi, gj, nk),
        in_specs=in_specs,
        out_specs=out_specs,
        out_shape=out_shape,
        scratch_shapes=[pltpu.VMEM((bm, bn), F32)] if nk > 1 else [],
        compiler_params=_params("parallel", "arbitrary", "arbitrary"),
        name=name,
    )(*args)
    main =tuple(outs[:n_main]) if gain is not None else outs[0]
    return main, list(outs[n_main:])


def _mix_kernel(z_ref, halo_ref, pw_ref, ps_ref, lg_ref, lb_ref, sw_ref, sb_ref,
                y_ref, v_ref, *, seq, wpool, wsgu):
    t = z_ref.shape[0]
    n_groups = len(POOL_WINDOWS)
    gc = wpool // n_groups
    pos0 = (pl.program_id(0) * t) % seq

    halo = halo_ref[...]
    halo = jnp.where(pos0 == 0, jnp.zeros_like(halo), halo)
    row = lax.broadcasted_iota(jnp.int32, (t, t + POOL_HALO), 0)
    col = lax.broadcasted_iota(jnp.int32, (t, t + POOL_HALO), 1)
    lag = row + POOL_HALO - col
    pos = pos0 + lax.broadcasted_iota(jnp.int32, (t, 1), 0)
    for gi, win in enumerate(POOL_WINDOWS):
        cs = slice(gi * gc, (gi + 1) * gc)
        zg = z_ref[:, cs]
        band = jnp.logical_and(lag >= 0, lag < win).astype(BF16)
        wsum = jnp.dot(band, jnp.concatenate([halo[:, cs], zg], axis=0),
                       preferred_element_type=F32)
        cnt = jnp.minimum(pos + 1, win).astype(F32)
        d = wsum / cnt - zg.astype(F32)
        yg = jnp.dot(d.astype(BF16), pw_ref[gi], preferred_element_type=F32)
        y_ref[:, cs] = (yg * ps_ref[:, cs]).astype(BF16)

    gv = jax.nn.gelu(z_ref[:, wpool + wsgu:].astype(F32))
    mu = jnp.mean(gv, axis=-1, keepdims=True)
    cen = gv - mu
    var = jnp.mean(cen * cen, axis=-1, keepdims=True)
    v_ref[...] = (cen * lax.rsqrt(var + EPS) * lg_ref[...] + lb_ref[...]).astype(BF16)

    tri = (lax.broadcasted_iota(jnp.int32, (CHUNK, CHUNK), 0)
           >= lax.broadcasted_iota(jnp.int32, (CHUNK, CHUNK), 1))
    for h in range(wsgu // SGU_HEAD):
        hs = slice(h * SGU_HEAD, (h + 1) * SGU_HEAD)
        us = slice(wpool + h * SGU_HEAD, wpool + (h + 1) * SGU_HEAD)
        wm = jnp.where(tri, sw_ref[h], jnp.zeros((CHUNK, CHUNK), BF16))
        for c in range(t // CHUNK):
            rs = slice(c * CHUNK, (c + 1) * CHUNK)
            mixed = jnp.dot(wm, v_ref[rs, hs], preferred_element_type=F32) + sb_ref[:, hs]
            u = jax.nn.gelu(z_ref[rs, us].astype(F32))
            y_ref[rs, us] = (u * mixed).astype(BF16)


def _mix(z, pool_w, pool_scale, ln_g, ln_b, sgu_w, sgu_b, *, layer, seq):
    m, d_in = z.shape
    _, n_groups, gc, _ = pool_w.shape
    wpool = n_groups * gc
    wsgu = (d_in - wpool) // 2
    n_heads = wsgu // SGU_HEAD
    assert n_groups == len(POOL_WINDOWS) and sgu_w.shape[1:] == (n_heads, CHUNK, CHUNK)
    t = _tile(seq, ROW_TILE)
    assert t % CHUNK == 0 and t % POOL_HALO == 0
    halo_blocks = t // POOL_HALO
    sb = jnp.repeat(jnp.transpose(sgu_b), SGU_HEAD, axis=1)
    kern = functools.partial(_mix_kernel, seq=seq, wpool=wpool, wsgu=wsgu)
    const2 = lambda i: (0, 0)
    return pl.pallas_call(
        kern,
        grid=(m // t,),
        in_specs=[pl.BlockSpec((t, d_in), lambda i: (i, 0)),
                  pl.BlockSpec((POOL_HALO, wpool),
                               lambda i: (jnp.maximum(i * halo_blocks - 1, 0), 0)),
                  pl.BlockSpec((None, n_groups, gc, gc), lambda i: (layer, 0, 0, 0)),
                  pl.BlockSpec((1, wpool), const2),
                  pl.BlockSpec((1, wsgu), const2),
                  pl.BlockSpec((1, wsgu), const2),
                  pl.BlockSpec((None, n_heads, CHUNK, CHUNK), lambda i: (layer, 0, 0, 0)),
                  pl.BlockSpec((CHUNK, wsgu), const2)],
        out_specs=pl.BlockSpec((t, wpool + wsgu), lambda i: (i, 0)),
        out_shape=jax.ShapeDtypeStruct((m, wpool + wsgu), BF16),
        scratch_shapes=[pltpu.VMEM((t, wsgu), BF16)],
        compiler_params=_params("parallel"),
        name="mix",
    )(z, z, pool_w, pool_scale.reshape(1, wpool), ln_g.reshape(1, wsgu),
      ln_b.reshape(1, wsgu), sgu_w, sb)


def _fold_qk_kernel(wq_ref, k_ref, a_ref):
    n_mem = a_ref.shape[2]
    s = lax.dot_general(wq_ref[...].astype(BF16), k_ref[...], (((1,), (1,)), ((), ())),
                        preferred_element_type=F32)
    for b in range(a_ref.shape[0]):
        a_ref[b] = s[:, b * n_mem:(b + 1) * n_mem].astype(a_ref.dtype)


def _fold_vo_kernel(v_ref, wo_ref, b_ref):
    n_mem = b_ref.shape[1]
    s = jnp.dot(v_ref[...], wo_ref[...].astype(BF16), preferred_element_type=F32)
    for b in range(b_ref.shape[0]):
        b_ref[b] = s[b * n_mem:(b + 1) * n_mem].astype(b_ref.dtype)


def _fold_qk(wq, k, *, layer, n_batch, n_mem):
    d = wq.shape[1]
    dh = d // N_XHEADS
    rb = _tile(d, FOLD_TILE)
    return pl.pallas_call(
        _fold_qk_kernel,
        grid=(N_XHEADS, d // rb),
        in_specs=[pl.BlockSpec((None, rb, dh), lambda h, r: (layer, r, h)),
                  pl.BlockSpec((n_batch * n_mem, dh), lambda h, r: (0, h))],
        out_specs=pl.BlockSpec((n_batch, rb, n_mem), lambda h, r: (0, r, h)),
        out_shape=jax.ShapeDtypeStruct((n_batch, d, N_XHEADS * n_mem), BF16),
        compiler_params=_params("parallel", "parallel"),
        name="fold_qk",
    )(wq, k)


def _fold_vo(v, wo, *, layer, n_batch, n_mem):
    d = wo.shape[2]
    dh = d // N_XHEADS
    cb = _tile(d, FOLD_TILE)
    return pl.pallas_call(
        _fold_vo_kernel,
        grid=(N_XHEADS, d // cb),
        in_specs=[pl.BlockSpec((n_batch * n_mem, dh), lambda h, c: (0, h)),
                  pl.BlockSpec((None, dh, cb), lambda h, c: (layer, h, c))],
        out_specs=pl.BlockSpec((n_batch, n_mem, cb), lambda h, c: (0, h, c)),
        out_shape=jax.ShapeDtypeStruct((n_batch, N_XHEADS * n_mem, d), BF16),
        compiler_params=_params("parallel", "parallel"),
        name="fold_vo",
    )(v, wo)


def kernel(x, mem, ln_mix, w_in, pool_w, pool_scale, sgu_ln_g, sgu_ln_b, sgu_w, sgu_b,
           w_out, ln_x, ln_mem, w_q, w_k, w_v, w_o, ln_ffn, w_up, w_down, ln_final):
    n_batch, seq, d = x.shape
    n_mem = mem.shape[1]
    depth = w_in.shape[0]
    dh = d // N_XHEADS
    x = x.reshape(n_batch * seq, d)
    memf = mem.reshape(n_batch * n_mem, d)

    wb_in = w_in[:1].astype(BF16)
    pool_w, sgu_w = pool_w.astype(BF16), sgu_w.astype(BF16)

    xb, ssq = _prep(x, ln_mix[0])
    for l in range(depth):
        z, (wb_out,) = _mm(xb, wb_in, layer=0, ssq=ssq, casts=[(w_out, l)], name="mm_in")
        y = _mix(z, pool_w, pool_scale[l], sgu_ln_g[l], sgu_ln_b[l], sgu_w, sgu_b[l],
                 layer=l, seq=seq)
        (x, xb, ssq), (wb_up,) = _mm(y, wb_out, layer=0, res=x, gain=ln_x[l], out_dtype=F32,
                                     casts=[(w_up, l)], bn=MM_BN_RES, name="mm_out")

        mb, mssq = _prep(memf, ln_mem[l])
        k, _ = _mm(mb, w_k, layer=l, ssq=mssq, bn=MM_BN_RES, name="mm_k")
        v, _ = _mm(mb, w_v, layer=l, ssq=mssq, bn=MM_BN_RES, name="mm_v")
        a_fold = _fold_qk(w_q, k, layer=l, n_batch=n_batch, n_mem=n_mem)
        b_fold = _fold_vo(v, w_o, layer=l, n_batch=n_batch, n_mem=n_mem)
        p, _ = _mm(xb, a_fold, ssq=ssq, post_scale=dh ** -0.5, act="group_softmax",
                   n_groups=N_XHEADS, bn=N_XHEADS * n_mem, name="mm_scores")
        (x, xb, ssq), _ = _mm(p, b_fold, res=x, gain=ln_ffn[l], out_dtype=F32,
                              name="mm_attn_out")

        last = l + 1 == depth
        hid, cast = _mm(xb, wb_up, layer=0, ssq=ssq, act="relu2",
                        casts=[(w_down, l)] + ([] if last else [(w_in, l + 1)]), name="mm_up")
        wb_down = cast[0]
        if last:
            x, _ = _mm(hid, wb_down, layer=0, res=x, out_dtype=F32, bk=MM_BK_DOWN,
                       name="mm_down")
        else:
            wb_in = cast[1]
            (x, xb, ssq), _ = _mm(hid, wb_down, layer=0, res=x, gain=ln_mix[l + 1],
                                  out_dtype=F32, bk=MM_BK_DOWN, name="mm_down")

    return _final_norm(x, ln_final).reshape(n_batch, seq, d)
```

```python
import functools

import jax
import jax.numpy as jnp
from jax import lax
from jax.experimental import pallas as pl
from jax.experimental.pallas import tpu as pltpu

EPS = 1e-6
POOL_WINDOWS = (2, 4, 8, 16)
POOL_HALO = 16
CHUNK = 128
SGU_HEAD = 128
N_XHEADS = 4
LANES = 128
BF16_SUBLANES = 16

V7X_VMEM_LIMIT_BYTES = 56 * 1024 * 1024

MM_BM = 1024
MM_BN = 1024
MM_BN_RES = 512
MM_BK = 4096
MM_BK_DOWN = 2048
ATTN_BM = 512
ROW_TILE = 256
FOLD_TILE = 1024

F32 = jnp.float32
BF16 = jnp.bfloat16


def _tile(dim, pref):
    t = min(dim, pref)
    assert dim % t == 0, (dim, pref)
    return t


def _params(*sem):
    return pltpu.CompilerParams(dimension_semantics=sem,
                                vmem_limit_bytes=V7X_VMEM_LIMIT_BYTES)


def _fold_lanes(v):
    acc = v[:, :LANES]
    for c in range(1, v.shape[1] // LANES):
        acc = acc + v[:, c * LANES:(c + 1) * LANES]
    return acc


def _prep_kernel(x_ref, g_ref, xb_ref, ssq_ref):
    x = x_ref[...]
    ssq_ref[...] = _fold_lanes(x * x)
    xb_ref[...] = (x * g_ref[...]).astype(BF16)


def _prep(x, g):
    m, d = x.shape
    bm = _tile(m, ROW_TILE)
    return pl.pallas_call(
        _prep_kernel,
        grid=(m // bm,),
        in_specs=[pl.BlockSpec((bm, d), lambda i: (i, 0)),
                  pl.BlockSpec((1, d), lambda i: (0, 0))],
        out_specs=[pl.BlockSpec((bm, d), lambda i: (i, 0)),
                   pl.BlockSpec((bm, LANES), lambda i: (i, 0))],
        out_shape=[jax.ShapeDtypeStruct((m, d), BF16),
                   jax.ShapeDtypeStruct((m, LANES), F32)],
        compiler_params=_params("parallel"),
        name="prep",
    )(x, g.reshape(1, d))


def _final_norm_kernel(x_ref, g_ref, o_ref):
    x = x_ref[...]
    y = x * lax.rsqrt(jnp.mean(x * x, axis=-1, keepdims=True) + EPS)
    o_ref[...] = y * g_ref[...]


def _final_norm(x, g):
    m, d = x.shape
    bm = _tile(m, ROW_TILE)
    return pl.pallas_call(
        _final_norm_kernel,
        grid=(m // bm,),
        in_specs=[pl.BlockSpec((bm, d), lambda i: (i, 0)),
                  pl.BlockSpec((1, d), lambda i: (0, 0))],
        out_specs=pl.BlockSpec((bm, d), lambda i: (i, 0)),
        out_shape=jax.ShapeDtypeStruct((m, d), F32),
        compiler_params=_params("parallel"),
        name="final_norm",
    )(x, g.reshape(1, d))


def _mm_kernel(*refs, nk, has_ssq, has_res, has_gain, n_casts, cast_w, act, inv_k):
    refs = list(refs)
    a_ref, w_ref = refs[0], refs[1]
    pos = 2
    ssq_ref = res_ref = gain_ref = None
    if has_ssq:
        ssq_ref = refs[pos]
        pos += 1
    if has_res:
        res_ref = refs[pos]
        pos += 1
    if has_gain:
        gain_ref = refs[pos]
        pos += 1
    cast_in = refs[pos:pos + n_casts]
    pos += n_casts
    o_ref = refs[pos]
    pos += 1
    xb_ref = ssq_out_ref = None
    if has_gain:
        xb_ref, ssq_out_ref = refs[pos], refs[pos + 1]
        pos += 2
    cast_out = refs[pos:pos + n_casts]
    pos += n_casts
    acc_ref = refs[pos] if nk > 1 else None

    for ci, co in zip(cast_in, cast_out):
        co[...] = ci[...].astype(BF16)

    def emit(new):
        xb_ref[...] = (new * gain_ref[...]).astype(BF16)
        ssq_out_ref[...] = _fold_lanes(new * new)

    def dot():
        w = w_ref[...]
        if cast_w:
            w = w.astype(BF16)
        return jnp.dot(a_ref[...], w, preferred_element_type=F32)

    if nk > 1:
        assert nk % 2 == 0 and has_res and not has_ssq and act is None
        k = pl.program_id(2)

        @pl.when(k == 0)
        def _():
            acc_ref[...] = res_ref[...] + dot()

        @pl.when(jnp.logical_and(k > 0, k % 2 == 0))
        def _():
            acc_ref[...] = o_ref[...] + dot()

        @pl.when(k % 2 == 1)
        def _():
            new = acc_ref[...] + dot()
            o_ref[...] = new
            if has_gain:
                emit(new)
        return

    acc = dot()
    if has_ssq:
        acc = acc * lax.rsqrt(jnp.sum(ssq_ref[...], axis=-1, keepdims=True) * inv_k + EPS)
    if act == "relu2":
        acc = jnp.square(jnp.maximum(acc, 0.0))
    if has_res:
        acc = acc + res_ref[...]
    o_ref[...] = acc.astype(o_ref.dtype)
    if has_gain:
        emit(acc)


def _mm(a, w, *, layer=None, ssq=None, res=None, act=None,
        gain=None, casts=(), out_dtype=BF16, bm=None, bn=None, bk=None, name="mm"):
    m, kdim = a.shape
    nl, _, n = w.shape
    rows_per_w = m if layer is not None else m // nl
    bm = _tile(rows_per_w, bm or MM_BM)
    bn = _tile(n, bn or MM_BN)
    bk = _tile(kdim, bk or MM_BK)
    nk = kdim // bk
    blocks_per_w = rows_per_w // bm
    if layer is not None:
        w_map = lambda i, j, k: (layer, k, j)
    else:
        w_map = lambda i, j, k: (i // blocks_per_w, k, j)

    in_specs = [pl.BlockSpec((bm, bk), lambda i, j, k: (i, k)),
                pl.BlockSpec((None, bk, bn), w_map)]
    args = [a, w]
    if ssq is not None:
        in_specs.append(pl.BlockSpec((bm, ssq.shape[1]), lambda i, j, k: (i, 0)))
        args.append(ssq)
    if res is not None:
        in_specs.append(pl.BlockSpec((bm, bn), lambda i, j, k: (i, j)))
        args.append(res)
    out_specs = [pl.BlockSpec((bm, bn), lambda i, j, k: (i, j))]
    out_shape = [jax.ShapeDtypeStruct((m, n), out_dtype)]
    if gain is not None:
        in_specs.append(pl.BlockSpec((1, bn), lambda i, j, k: (0, j)))
        args.append(gain.reshape(1, n))
        out_specs += [pl.BlockSpec((bm, bn), lambda i, j, k: (i, j)),
                      pl.BlockSpec((bm, LANES), lambda i, j, k: (i, j))]
        out_shape += [jax.ShapeDtypeStruct((m, n), BF16),
                      jax.ShapeDtypeStruct((m, LANES * (n // bn)), F32)]
    n_main = len(out_shape)
    gi, gj = m // bm, n // bn
    for src, src_layer in casts:
        _, kw, nw = src.shape
        cj = max(c for c in range(1, gj + 1)
                 if kw % (gi * c) == 0 and (kw // (gi * c)) % BF16_SUBLANES == 0)
        rows = kw // (gi * cj)
        slab = lambda i, j, k, cj=cj: i * cj + jnp.minimum(j, cj - 1)
        in_specs.append(pl.BlockSpec(
            (None, rows, nw), lambda i, j, k, sl=slab, l=src_layer: (l, sl(i, j, k), 0)))
        args.append(src)
        out_specs.append(pl.BlockSpec(
            (None, rows, nw), lambda i, j, k, sl=slab: (0, sl(i, j, k), 0)))
        out_shape.append(jax.ShapeDtypeStruct((1, kw, nw), BF16))
    kern = functools.partial(
        _mm_kernel, nk=nk, has_ssq=ssq is not None, has_res=res is not None,
        has_gain=gain is not None, n_casts=len(casts), cast_w=w.dtype != BF16, act=act,
        inv_k=1.0 / kdim)
    outs = pl.pallas_call(
        kern,
        grid=(gi, gj, nk),
        in_specs=in_specs,
        out_specs=out_specs,
        out_shape=out_shape,
        scratch_shapes=[pltpu.VMEM((bm, bn), F32)] if nk > 1 else [],
        compiler_params=_params("parallel", "arbitrary", "arbitrary"),
        name=name,
    )(*args)
    main =tuple(outs[:n_main]) if gain is not None else outs[0]
    return main, list(outs[n_main:])


def _mix_kernel(z_ref, halo_ref, pw_ref, ps_ref, lg_ref, lb_ref, sw_ref, sb_ref,
                y_ref, v_ref, *, seq, wpool, wsgu):
    t = z_ref.shape[0]
    n_groups = len(POOL_WINDOWS)
    gc = wpool // n_groups
    pos0 = (pl.program_id(0) * t) % seq

    halo = halo_ref[...]
    halo = jnp.where(pos0 == 0, jnp.zeros_like(halo), halo)
    row = lax.broadcasted_iota(jnp.int32, (t, t + POOL_HALO), 0)
    col = lax.broadcasted_iota(jnp.int32, (t, t + POOL_HALO), 1)
    lag = row + POOL_HALO - col
    pos = pos0 + lax.broadcasted_iota(jnp.int32, (t, 1), 0)
    for gi, win in enumerate(POOL_WINDOWS):
        cs = slice(gi * gc, (gi + 1) * gc)
        zg = z_ref[:, cs]
        band = jnp.logical_and(lag >= 0, lag < win).astype(BF16)
        wsum = jnp.dot(band, jnp.concatenate([halo[:, cs], zg], axis=0),
                       preferred_element_type=F32)
        cnt = jnp.minimum(pos + 1, win).astype(F32)
        d = wsum / cnt - zg.astype(F32)
        yg = jnp.dot(d.astype(BF16), pw_ref[gi], preferred_element_type=F32)
        y_ref[:, cs] = (yg * ps_ref[:, cs]).astype(BF16)

    gv = jax.nn.gelu(z_ref[:, wpool + wsgu:].astype(F32))
    mu = jnp.mean(gv, axis=-1, keepdims=True)
    cen = gv - mu
    var = jnp.mean(cen * cen, axis=-1, keepdims=True)
    v_ref[...] = (cen * lax.rsqrt(var + EPS) * lg_ref[...] + lb_ref[...]).astype(BF16)

    tri = (lax.broadcasted_iota(jnp.int32, (CHUNK, CHUNK), 0)
           >= lax.broadcasted_iota(jnp.int32, (CHUNK, CHUNK), 1))
    for h in range(wsgu // SGU_HEAD):
        hs = slice(h * SGU_HEAD, (h + 1) * SGU_HEAD)
        us = slice(wpool + h * SGU_HEAD, wpool + (h + 1) * SGU_HEAD)
        wm = jnp.where(tri, sw_ref[h], jnp.zeros((CHUNK, CHUNK), BF16))
        for c in range(t // CHUNK):
            rs = slice(c * CHUNK, (c + 1) * CHUNK)
            mixed = jnp.dot(wm, v_ref[rs, hs], preferred_element_type=F32) + sb_ref[:, hs]
            u = jax.nn.gelu(z_ref[rs, us].astype(F32))
            y_ref[rs, us] = (u * mixed).astype(BF16)


def _mix(z, pool_w, pool_scale, ln_g, ln_b, sgu_w, sgu_b, *, layer, seq):
    m, d_in = z.shape
    _, n_groups, gc, _ = pool_w.shape
    wpool = n_groups * gc
    wsgu = (d_in - wpool) // 2
    n_heads = wsgu // SGU_HEAD
    assert n_groups == len(POOL_WINDOWS) and sgu_w.shape[1:] == (n_heads, CHUNK, CHUNK)
    t = _tile(seq, ROW_TILE)
    assert t % CHUNK == 0 and t % POOL_HALO == 0
    halo_blocks = t // POOL_HALO
    sb = jnp.repeat(jnp.transpose(sgu_b), SGU_HEAD, axis=1)
    kern = functools.partial(_mix_kernel, seq=seq, wpool=wpool, wsgu=wsgu)
    const2 = lambda i: (0, 0)
    return pl.pallas_call(
        kern,
        grid=(m // t,),
        in_specs=[pl.BlockSpec((t, d_in), lambda i: (i, 0)),
                  pl.BlockSpec((POOL_HALO, wpool),
                               lambda i: (jnp.maximum(i * halo_blocks - 1, 0), 0)),
                  pl.BlockSpec((None, n_groups, gc, gc), lambda i: (layer, 0, 0, 0)),
                  pl.BlockSpec((1, wpool), const2),
                  pl.BlockSpec((1, wsgu), const2),
                  pl.BlockSpec((1, wsgu), const2),
                  pl.BlockSpec((None, n_heads, CHUNK, CHUNK), lambda i: (layer, 0, 0, 0)),
                  pl.BlockSpec((CHUNK, wsgu), const2)],
        out_specs=pl.BlockSpec((t, wpool + wsgu), lambda i: (i, 0)),
        out_shape=jax.ShapeDtypeStruct((m, wpool + wsgu), BF16),
        scratch_shapes=[pltpu.VMEM((t, wsgu), BF16)],
        compiler_params=_params("parallel"),
        name="mix",
    )(z, z, pool_w, pool_scale.reshape(1, wpool), ln_g.reshape(1, wsgu),
      ln_b.reshape(1, wsgu), sgu_w, sb)


def _fold_qk_kernel(wq_ref, k_ref, a_ref):
    n_mem = a_ref.shape[2]
    s = lax.dot_general(wq_ref[...].astype(BF16), k_ref[...], (((1,), (1,)), ((), ())),
                        preferred_element_type=F32)
    for b in range(a_ref.shape[0]):
        a_ref[b] = s[:, b * n_mem:(b + 1) * n_mem].astype(a_ref.dtype)


def _fold_vo_kernel(v_ref, wo_ref, b_ref):
    n_mem = b_ref.shape[1]
    s = jnp.dot(v_ref[...], wo_ref[...].astype(BF16), preferred_element_type=F32)
    for b in range(b_ref.shape[0]):
        b_ref[b] = s[b * n_mem:(b + 1) * n_mem].astype(b_ref.dtype)


def _fold_qk(wq, k, *, layer, n_batch, n_mem):
    d = wq.shape[1]
    dh = d // N_XHEADS
    rb = _tile(d, FOLD_TILE)
    return pl.pallas_call(
        _fold_qk_kernel,
        grid=(N_XHEADS, d // rb),
        in_specs=[pl.BlockSpec((None, rb, dh), lambda h, r: (layer, r, h)),
                  pl.BlockSpec((n_batch * n_mem, dh), lambda h, r: (0, h))],
        out_specs=pl.BlockSpec((n_batch, rb, n_mem), lambda h, r: (0, r, h)),
        out_shape=jax.ShapeDtypeStruct((n_batch, d, N_XHEADS * n_mem), BF16),
        compiler_params=_params("parallel", "parallel"),
        name="fold_qk",
    )(wq, k)


def _fold_vo(v, wo, *, layer, n_batch, n_mem):
    d = wo.shape[2]
    dh = d // N_XHEADS
    cb = _tile(d, FOLD_TILE)
    return pl.pallas_call(
        _fold_vo_kernel,
        grid=(N_XHEADS, d // cb),
        in_specs=[pl.BlockSpec((n_batch * n_mem, dh), lambda h, c: (0, h)),
                  pl.BlockSpec((None, dh, cb), lambda h, c: (layer, h, c))],
        out_specs=pl.BlockSpec((n_batch, n_mem, cb), lambda h, c: (0, h, c)),
        out_shape=jax.ShapeDtypeStruct((n_batch, N_XHEADS * n_mem, d), BF16),
        compiler_params=_params("parallel", "parallel"),
        name="fold_vo",
    )(v, wo)


def _attn_kernel(xb_ref, ssq_ref, a_ref, bm_ref, res_ref, gain_ref, o_ref, xbo_ref, ssqo_ref,
                 p_ref, *, n_mem, score_scale, inv_k):
    @pl.when(pl.program_id(1) == 0)
    def _():
        r = lax.rsqrt(jnp.sum(ssq_ref[...], axis=-1, keepdims=True) * inv_k + EPS)
        s = jnp.dot(xb_ref[...], a_ref[...], preferred_element_type=F32) * (r * score_scale)
        for h in range(s.shape[1] // n_mem):
            sh = s[:, h * n_mem:(h + 1) * n_mem]
            e = jnp.exp(sh - jnp.max(sh, axis=-1, keepdims=True))
            p_ref[:, h * n_mem:(h + 1) * n_mem] = (
                e / jnp.sum(e, axis=-1, keepdims=True)).astype(BF16)

    new = jnp.dot(p_ref[...], bm_ref[...], preferred_element_type=F32) + res_ref[...]
    o_ref[...] = new
    xbo_ref[...] = (new * gain_ref[...]).astype(BF16)
    ssqo_ref[...] = _fold_lanes(new * new)


def _attn(xb, ssq, a_fold, b_fold, res, gain, *, n_mem, score_scale):
    m, d = xb.shape
    n_batch, _, hm = a_fold.shape
    bm = _tile(m // n_batch, ATTN_BM)
    bn = _tile(d, MM_BN)
    blocks_per_batch = m // n_batch // bm
    kern = functools.partial(_attn_kernel, n_mem=n_mem, score_scale=score_scale, inv_k=1.0 / d)
    return pl.pallas_call(
        kern,
        grid=(m // bm, d // bn),
        in_specs=[pl.BlockSpec((bm, d), lambda i, j: (i, 0)),
                  pl.BlockSpec((bm, ssq.shape[1]), lambda i, j: (i, 0)),
                  pl.BlockSpec((None, d, hm), lambda i, j: (i // blocks_per_batch, 0, 0)),
                  pl.BlockSpec((None, hm, bn), lambda i, j: (i // blocks_per_batch, 0, j)),
                  pl.BlockSpec((bm, bn), lambda i, j: (i, j)),
                  pl.BlockSpec((1, bn), lambda i, j: (0, j))],
        out_specs=[pl.BlockSpec((bm, bn), lambda i, j: (i, j)),
                   pl.BlockSpec((bm, bn), lambda i, j: (i, j)),
                   pl.BlockSpec((bm, LANES), lambda i, j: (i, j))],
        out_shape=[jax.ShapeDtypeStruct((m, d), F32),
                   jax.ShapeDtypeStruct((m, d), BF16),
                   jax.ShapeDtypeStruct((m, LANES * (d // bn)), F32)],
        scratch_shapes=[pltpu.VMEM((bm, hm), BF16)],
        compiler_params=_params("parallel", "arbitrary"),
        name="attn",
    )(xb, ssq, a_fold, b_fold, res, gain.reshape(1, d))


def kernel(x, mem, ln_mix, w_in, pool_w, pool_scale, sgu_ln_g, sgu_ln_b, sgu_w, sgu_b,
           w_out, ln_x, ln_mem, w_q, w_k, w_v, w_o, ln_ffn, w_up, w_down, ln_final):
    n_batch, seq, d = x.shape
    n_mem = mem.shape[1]
    depth = w_in.shape[0]
    dh = d // N_XHEADS
    x = x.reshape(n_batch * seq, d)
    memf = mem.reshape(n_batch * n_mem, d)

    wb_in = w_in[:1].astype(BF16)
    pool_w, sgu_w = pool_w.astype(BF16), sgu_w.astype(BF16)

    xb, ssq = _prep(x, ln_mix[0])
    for l in range(depth):
        z, (wb_out,) = _mm(xb, wb_in, layer=0, ssq=ssq, casts=[(w_out, l)], name="mm_in")
        y = _mix(z, pool_w, pool_scale[l], sgu_ln_g[l], sgu_ln_b[l], sgu_w, sgu_b[l],
                 layer=l, seq=seq)
        (x, xb, ssq), (wb_up,) = _mm(y, wb_out, layer=0, res=x, gain=ln_x[l], out_dtype=F32,
                                     casts=[(w_up, l)], bn=MM_BN_RES, name="mm_out")

        mb, mssq = _prep(memf, ln_mem[l])
        k, _ = _mm(mb, w_k, layer=l, ssq=mssq, bn=MM_BN_RES, name="mm_k")
        v, _ = _mm(mb, w_v, layer=l, ssq=mssq, bn=MM_BN_RES, name="mm_v")
        a_fold = _fold_qk(w_q, k, layer=l, n_batch=n_batch, n_mem=n_mem)
        b_fold = _fold_vo(v, w_o, layer=l, n_batch=n_batch, n_mem=n_mem)
        x, xb, ssq = _attn(xb, ssq, a_fold, b_fold, x, ln_ffn[l], n_mem=n_mem,
                           score_scale=dh ** -0.5)

        last = l + 1 == depth
        hid, cast = _mm(xb, wb_up, layer=0, ssq=ssq, act="relu2",
                        casts=[(w_down, l)] + ([] if last else [(w_in, l + 1)]), name="mm_up")
        wb_down = cast[0]
        if last:
            x, _ = _mm(hid, wb_down, layer=0, res=x, out_dtype=F32, bk=MM_BK_DOWN,
                       name="mm_down")
        else:
            wb_in = cast[1]
            (x, xb, ssq), _ = _mm(hid, wb_down, layer=0, res=x, gain=ln_mix[l + 1],
                                  out_dtype=F32, bk=MM_BK_DOWN, name="mm_down")

    return _final_norm(x, ln_final).reshape(n_batch, seq, d)
```

```python
import functools

import jax
import jax.numpy as jnp
from jax import lax
from jax.experimental import pallas as pl
from jax.experimental.pallas import tpu as pltpu

EPS = 1e-6
POOL_WINDOWS = (2, 4, 8, 16)
POOL_HALO = 16
CHUNK = 128
SGU_HEAD = 128
N_XHEADS = 4
LANES = 128
BF16_SUBLANES = 16

V7X_VMEM_LIMIT_BYTES = 56 * 1024 * 1024

MM_BM = 1024
MM_BN = 1024
MM_BN_RES = 512
MM_BK = 4096
MM_BK_DOWN = 2048
ROW_TILE = 256
FOLD_TILE = 1024

F32 = jnp.float32
BF16 = jnp.bfloat16


def _tile(dim, pref):
    t = min(dim, pref)
    assert dim % t == 0, (dim, pref)
    return t


def _params(*sem):
    return pltpu.CompilerParams(dimension_semantics=sem,
                                vmem_limit_bytes=V7X_VMEM_LIMIT_BYTES)


def _fold_lanes(v):
    acc = v[:, :LANES]
    for c in range(1, v.shape[1] // LANES):
        acc = acc + v[:, c * LANES:(c + 1) * LANES]
    return acc


def _prep_kernel(x_ref, g_ref, xb_ref, ssq_ref):
    x = x_ref[...]
    ssq_ref[...] = _fold_lanes(x * x)
    xb_ref[...] = (x * g_ref[...]).astype(BF16)


def _prep(x, g):
    m, d = x.shape
    bm = _tile(m, ROW_TILE)
    return pl.pallas_call(
        _prep_kernel,
        grid=(m // bm,),
        in_specs=[pl.BlockSpec((bm, d), lambda i: (i, 0)),
                  pl.BlockSpec((1, d), lambda i: (0, 0))],
        out_specs=[pl.BlockSpec((bm, d), lambda i: (i, 0)),
                   pl.BlockSpec((bm, LANES), lambda i: (i, 0))],
        out_shape=[jax.ShapeDtypeStruct((m, d), BF16),
                   jax.ShapeDtypeStruct((m, LANES), F32)],
        compiler_params=_params("parallel"),
        name="prep",
    )(x, g.reshape(1, d))


def _final_norm_kernel(x_ref, g_ref, o_ref):
    x = x_ref[...]
    y = x * lax.rsqrt(jnp.mean(x * x, axis=-1, keepdims=True) + EPS)
    o_ref[...] = y * g_ref[...]


def _final_norm(x, g):
    m, d = x.shape
    bm = _tile(m, ROW_TILE)
    return pl.pallas_call(
        _final_norm_kernel,
        grid=(m // bm,),
        in_specs=[pl.BlockSpec((bm, d), lambda i: (i, 0)),
                  pl.BlockSpec((1, d), lambda i: (0, 0))],
        out_specs=pl.BlockSpec((bm, d), lambda i: (i, 0)),
        out_shape=jax.ShapeDtypeStruct((m, d), F32),
        compiler_params=_params("parallel"),
        name="final_norm",
    )(x, g.reshape(1, d))


def _mm_kernel(*refs, nk, has_ssq, has_res, has_gain, n_casts, cast_w, act, n_groups,
               post_scale, inv_k):
    refs = list(refs)
    a_ref, w_ref = refs[0], refs[1]
    pos = 2
    ssq_ref = res_ref = gain_ref = None
    if has_ssq:
        ssq_ref = refs[pos]
        pos += 1
    if has_res:
        res_ref = refs[pos]
        pos += 1
    if has_gain:
        gain_ref = refs[pos]
        pos += 1
    cast_in = refs[pos:pos + n_casts]
    pos += n_casts
    o_ref = refs[pos]
    pos += 1
    xb_ref = ssq_out_ref = None
    if has_gain:
        xb_ref, ssq_out_ref = refs[pos], refs[pos + 1]
        pos += 2
    cast_out = refs[pos:pos + n_casts]
    pos += n_casts
    acc_ref = refs[pos] if nk > 1 else None

    for ci, co in zip(cast_in, cast_out):
        co[...] = ci[...].astype(BF16)

    def emit(new):
        xb_ref[...] = (new * gain_ref[...]).astype(BF16)
        ssq_out_ref[...] = _fold_lanes(new * new)

    def dot():
        w = w_ref[...]
        if cast_w:
            w = w.astype(BF16)
        return jnp.dot(a_ref[...], w, preferred_element_type=F32)

    if nk > 1:
        assert nk % 2 == 0 and has_res and not has_ssq and act is None and post_scale is None
        k = pl.program_id(2)

        @pl.when(k == 0)
        def _():
            acc_ref[...] = res_ref[...] + dot()

        @pl.when(jnp.logical_and(k > 0, k % 2 == 0))
        def _():
            acc_ref[...] = o_ref[...] + dot()

        @pl.when(k % 2 == 1)
        def _():
            new = acc_ref[...] + dot()
            o_ref[...] = new
            if has_gain:
                emit(new)
        return

    acc = dot()
    if has_ssq:
        acc = acc * lax.rsqrt(jnp.sum(ssq_ref[...], axis=-1, keepdims=True) * inv_k + EPS)
    if post_scale is not None:
        acc = acc * post_scale
    if act == "relu2":
        acc = jnp.square(jnp.maximum(acc, 0.0))
    elif act == "group_softmax":
        gw = acc.shape[1] // n_groups
        parts = []
        for gi in range(n_groups):
            s = acc[:, gi * gw:(gi + 1) * gw]
            e = jnp.exp(s - jnp.max(s, axis=-1, keepdims=True))
            parts.append(e / jnp.sum(e, axis=-1, keepdims=True))
        acc = jnp.concatenate(parts, axis=1)
    if has_res:
        acc = acc + res_ref[...]
    o_ref[...] = acc.astype(o_ref.dtype)
    if has_gain:
        emit(acc)


def _mm(a, w, *, layer=None, ssq=None, post_scale=None, res=None, act=None, n_groups=1,
        gain=None, casts=(), out_dtype=BF16, bm=None, bn=None, bk=None, name="mm"):
    m, kdim = a.shape
    nl, _, n = w.shape
    rows_per_w = m if layer is not None else m // nl
    bm = _tile(rows_per_w, bm or MM_BM)
    bn = _tile(n, bn or MM_BN)
    bk = _tile(kdim, bk or MM_BK)
    nk = kdim // bk
    blocks_per_w = rows_per_w // bm
    if layer is not None:
        w_map = lambda i, j, k: (layer, k, j)
    else:
        w_map = lambda i, j, k: (i // blocks_per_w, k, j)

    in_specs = [pl.BlockSpec((bm, bk), lambda i, j, k: (i, k)),
                pl.BlockSpec((None, bk, bn), w_map)]
    args = [a, w]
    if ssq is not None:
        in_specs.append(pl.BlockSpec((bm, ssq.shape[1]), lambda i, j, k: (i, 0)))
        args.append(ssq)
    if res is not None:
        in_specs.append(pl.BlockSpec((bm, bn), lambda i, j, k: (i, j)))
        args.append(res)
    out_specs = [pl.BlockSpec((bm, bn), lambda i, j, k: (i, j))]
    out_shape = [jax.ShapeDtypeStruct((m, n), out_dtype)]
    if gain is not None:
        in_specs.append(pl.BlockSpec((1, bn), lambda i, j, k: (0, j)))
        args.append(gain.reshape(1, n))
        out_specs += [pl.BlockSpec((bm, bn), lambda i, j, k: (i, j)),
                      pl.BlockSpec((bm, LANES), lambda i, j, k: (i, j))]
        out_shape += [jax.ShapeDtypeStruct((m, n), BF16),
                      jax.ShapeDtypeStruct((m, LANES * (n // bn)), F32)]
    n_main = len(out_shape)
    gi, gj = m // bm, n // bn
    for src, src_layer in casts:
        _, kw, nw = src.shape
        cj = max(c for c in range(1, gj + 1)
                 if kw % (gi * c) == 0 and (kw // (gi * c)) % BF16_SUBLANES == 0)
        rows = kw // (gi * cj)
        slab = lambda i, j, k, cj=cj: i * cj + jnp.minimum(j, cj - 1)
        in_specs.append(pl.BlockSpec(
            (None, rows, nw), lambda i, j, k, sl=slab, l=src_layer: (l, sl(i, j, k), 0)))
        args.append(src)
        out_specs.append(pl.BlockSpec(
            (None, rows, nw), lambda i, j, k, sl=slab: (0, sl(i, j, k), 0)))
        out_shape.append(jax.ShapeDtypeStruct((1, kw, nw), BF16))
    kern = functools.partial(
        _mm_kernel, nk=nk, has_ssq=ssq is not None, has_res=res is not None,
        has_gain=gain is not None, n_casts=len(casts), cast_w=w.dtype != BF16, act=act,
        n_groups=n_groups, post_scale=post_scale, inv_k=1.0 / kdim)
    outs = pl.pallas_call(
        kern,
        grid=(gi, gj, nk),
        in_specs=in_specs,
        out_specs=out_specs,
        out_shape=out_shape,
        scratch_shapes=[pltpu.VMEM((bm, bn), F32)] if nk > 1 else [],
        compiler_params=_params("parallel", "arbitrary", "arbitrary"),
        name=name,
    )(*args)
    main =tuple(outs[:n_main]) if gain is not None else outs[0]
    return main, list(outs[n_main:])


def _mix_rows(z_ref, halo_ref, pw_ref, ps_ref, lg_ref, lb_ref, sw_ref, sb_ref, y_ref, v_ref,
              pos0, wpool, wsgu):
    t = z_ref.shape[0]
    n_groups = len(POOL_WINDOWS)
    gc = wpool // n_groups

    halo = halo_ref[...]
    halo = jnp.where(pos0 == 0, jnp.zeros_like(halo), halo)
    row = lax.broadcasted_iota(jnp.int32, (t, t + POOL_HALO), 0)
    col = lax.broadcasted_iota(jnp.int32, (t, t + POOL_HALO), 1)
    lag = row + POOL_HALO - col
    pos = pos0 + lax.broadcasted_iota(jnp.int32, (t, 1), 0)
    for gi, win in enumerate(POOL_WINDOWS):
        cs = slice(gi * gc, (gi + 1) * gc)
        zg = z_ref[:, cs]
        band = jnp.logical_and(lag >= 0, lag < win).astype(BF16)
        wsum = jnp.dot(band, jnp.concatenate([halo[:, cs], zg], axis=0),
                       preferred_element_type=F32)
        cnt = jnp.minimum(pos + 1, win).astype(F32)
        d = wsum / cnt - zg.astype(F32)
        yg = jnp.dot(d.astype(BF16), pw_ref[gi], preferred_element_type=F32)
        y_ref[:, cs] = (yg * ps_ref[:, cs]).astype(BF16)

    gv = jax.nn.gelu(z_ref[:, wpool + wsgu:].astype(F32))
    mu = jnp.mean(gv, axis=-1, keepdims=True)
    cen = gv - mu
    var = jnp.mean(cen * cen, axis=-1, keepdims=True)
    v_ref[...] = (cen * lax.rsqrt(var + EPS) * lg_ref[...] + lb_ref[...]).astype(BF16)

    tri = (lax.broadcasted_iota(jnp.int32, (CHUNK, CHUNK), 0)
           >= lax.broadcasted_iota(jnp.int32, (CHUNK, CHUNK), 1))
    for h in range(wsgu // SGU_HEAD):
        hs = slice(h * SGU_HEAD, (h + 1) * SGU_HEAD)
        us = slice(wpool + h * SGU_HEAD, wpool + (h + 1) * SGU_HEAD)
        wm = jnp.where(tri, sw_ref[h], jnp.zeros((CHUNK, CHUNK), BF16))
        for c in range(t // CHUNK):
            rs = slice(c * CHUNK, (c + 1) * CHUNK)
            mixed = jnp.dot(wm, v_ref[rs, hs], preferred_element_type=F32) + sb_ref[:, hs]
            u = jax.nn.gelu(z_ref[rs, us].astype(F32))
            y_ref[rs, us] = (u * mixed).astype(BF16)


def _mixout_kernel(*refs, seq, wpool, wsgu, n_tiles, has_cast):
    (z_ref, halo_ref, pw_ref, ps_ref, lg_ref, lb_ref, sw_ref, sb_ref,
     w_ref, res_ref, gain_ref, y0_hbm) = refs[:12]
    pos = 12
    cast_in = cast_out = None
    if has_cast:
        cast_in = refs[pos]
        pos += 1
    o_ref, xb_ref, ssq_out_ref = refs[pos:pos + 3]
    pos += 3
    if has_cast:
        cast_out = refs[pos]
        pos += 1
    y_even, y_odd, v_ref, y0_sem = refs[pos:pos + 4]
    i, j = pl.program_id(0), pl.program_id(1)
    bm = y_even.shape[0]

    if has_cast:
        cast_out[...] = cast_in[...].astype(BF16)

    @pl.when(jnp.logical_and(i == 0, j == 0))
    def _():
        y0_copy = pltpu.make_async_copy(y0_hbm, y_even, y0_sem)
        y0_copy.start()
        y0_copy.wait()

    row0 = jnp.minimum(i + 1, n_tiles - 1) * bm + j * CHUNK

    def step(y_read, y_write):
        chunk = y_write.at[pl.ds(pl.multiple_of(j * CHUNK, CHUNK), CHUNK), :]
        _mix_rows(z_ref, halo_ref, pw_ref, ps_ref, lg_ref, lb_ref, sw_ref, sb_ref, chunk,
                  v_ref, row0 % seq, wpool, wsgu)
        new = jnp.dot(y_read[...], w_ref[...], preferred_element_type=F32) + res_ref[...]
        o_ref[...] = new
        xb_ref[...] = (new * gain_ref[...]).astype(BF16)
        ssq_out_ref[...] = _fold_lanes(new * new)

    @pl.when(i % 2 == 0)
    def _():
        step(y_even, y_odd)

    @pl.when(i % 2 == 1)
    def _():
        step(y_odd, y_even)


def _mixout(z, pool_w, pool_scale, ln_g, ln_b, sgu_w, sgu_b, w, res, gain, *, layer, seq,
            cast=None):
    m, d_in = z.shape
    _, n_groups, gc, _ = pool_w.shape
    wpool = n_groups * gc
    wsgu = (d_in - wpool) // 2
    n_heads = wsgu // SGU_HEAD
    assert n_groups == len(POOL_WINDOWS) and sgu_w.shape[1:] == (n_heads, CHUNK, CHUNK)
    _, kdim, n = w.shape
    assert kdim == wpool + wsgu
    bm = _tile(seq, MM_BM)
    gj = bm // CHUNK
    bn = n // gj
    n_tiles = m // bm
    assert bm % CHUNK == 0 and n % gj == 0 and bn % LANES == 0
    halo_blocks = CHUNK // POOL_HALO
    sb = jnp.repeat(jnp.transpose(sgu_b), SGU_HEAD, axis=1)

    y0 = _mix(z, pool_w, pool_scale, ln_g, ln_b, sgu_w, sgu_b, layer=layer, seq=seq, rows=bm)

    chunk_idx = lambda i, j: jnp.minimum(i + 1, n_tiles - 1) * gj + j
    const2 = lambda i, j: (0, 0)
    in_specs = [pl.BlockSpec((CHUNK, d_in), lambda i, j: (chunk_idx(i, j), 0)),
                pl.BlockSpec((POOL_HALO, wpool),
                             lambda i, j: (jnp.maximum(chunk_idx(i, j) * halo_blocks - 1, 0), 0)),
                pl.BlockSpec((None, n_groups, gc, gc), lambda i, j: (layer, 0, 0, 0)),
                pl.BlockSpec((1, wpool), const2),
                pl.BlockSpec((1, wsgu), const2),
                pl.BlockSpec((1, wsgu), const2),
                pl.BlockSpec((None, n_heads, CHUNK, CHUNK), lambda i, j: (layer, 0, 0, 0)),
                pl.BlockSpec((CHUNK, wsgu), const2),
                pl.BlockSpec((None, kdim, bn), lambda i, j: (0, 0, j)),
                pl.BlockSpec((bm, bn), lambda i, j: (i, j)),
                pl.BlockSpec((1, bn), lambda i, j: (0, j)),
                pl.BlockSpec(memory_space=pl.ANY)]
    args = [z, z, pool_w, pool_scale.reshape(1, wpool), ln_g.reshape(1, wsgu),
            ln_b.reshape(1, wsgu), sgu_w, sb, w, res, gain.reshape(1, n), y0]
    out_specs = [pl.BlockSpec((bm, bn), lambda i, j: (i, j)),
                 pl.BlockSpec((bm, bn), lambda i, j: (i, j)),
                 pl.BlockSpec((bm, LANES), lambda i, j: (i, j))]
    out_shape = [jax.ShapeDtypeStruct((m, n), F32),
                 jax.ShapeDtypeStruct((m, n), BF16),
                 jax.ShapeDtypeStruct((m, LANES * gj), F32)]
    if cast is not None:
        src, src_layer = cast
        _, kw, nw = src.shape
        steps = n_tiles * gj
        cs = max(c for c in range(1, steps + 1)
                 if kw % c == 0 and (kw // c) % BF16_SUBLANES == 0)
        slab = lambda i, j: jnp.minimum(i * gj + j, cs - 1)
        in_specs.append(pl.BlockSpec((None, kw // cs, nw),
                                     lambda i, j: (src_layer, slab(i, j), 0)))
        args.append(src)
        out_specs.append(pl.BlockSpec((None, kw // cs, nw), lambda i, j: (0, slab(i, j), 0)))
        out_shape.append(jax.ShapeDtypeStruct((1, kw, nw), BF16))
    kern = functools.partial(_mixout_kernel, seq=seq, wpool=wpool, wsgu=wsgu, n_tiles=n_tiles,
                             has_cast=cast is not None)
    outs = pl.pallas_call(
        kern,
        grid=(n_tiles, gj),
        in_specs=in_specs,
        out_specs=out_specs,
        out_shape=out_shape,
        scratch_shapes=[pltpu.VMEM((bm, kdim), BF16), pltpu.VMEM((bm, kdim), BF16),
                        pltpu.VMEM((CHUNK, wsgu), BF16), pltpu.SemaphoreType.DMA(())],
        compiler_params=_params("arbitrary", "arbitrary"),
        name="mixout",
    )(*args)
    return outs[0], outs[1], outs[2], (outs[3] if cast is not None else None)


def _mix_kernel(z_ref, halo_ref, pw_ref, ps_ref, lg_ref, lb_ref, sw_ref, sb_ref,
                y_ref, v_ref, *, seq, wpool, wsgu):
    pos0 = (pl.program_id(0) * z_ref.shape[0]) % seq
    _mix_rows(z_ref, halo_ref, pw_ref, ps_ref, lg_ref, lb_ref, sw_ref, sb_ref, y_ref, v_ref,
              pos0, wpool, wsgu)


def _mix(z, pool_w, pool_scale, ln_g, ln_b, sgu_w, sgu_b, *, layer, seq, rows):
    m, d_in = rows, z.shape[1]
    _, n_groups, gc, _ = pool_w.shape
    wpool = n_groups * gc
    wsgu = (d_in - wpool) // 2
    n_heads = wsgu // SGU_HEAD
    assert n_groups == len(POOL_WINDOWS) and sgu_w.shape[1:] == (n_heads, CHUNK, CHUNK)
    t = _tile(seq, ROW_TILE)
    assert t % CHUNK == 0 and t % POOL_HALO == 0
    halo_blocks = t // POOL_HALO
    sb = jnp.repeat(jnp.transpose(sgu_b), SGU_HEAD, axis=1)
    kern = functools.partial(_mix_kernel, seq=seq, wpool=wpool, wsgu=wsgu)
    const2 = lambda i: (0, 0)
    return pl.pallas_call(
        kern,
        grid=(m // t,),
        in_specs=[pl.BlockSpec((t, d_in), lambda i: (i, 0)),
                  pl.BlockSpec((POOL_HALO, wpool),
                               lambda i: (jnp.maximum(i * halo_blocks - 1, 0), 0)),
                  pl.BlockSpec((None, n_groups, gc, gc), lambda i: (layer, 0, 0, 0)),
                  pl.BlockSpec((1, wpool), const2),
                  pl.BlockSpec((1, wsgu), const2),
                  pl.BlockSpec((1, wsgu), const2),
                  pl.BlockSpec((None, n_heads, CHUNK, CHUNK), lambda i: (layer, 0, 0, 0)),
                  pl.BlockSpec((CHUNK, wsgu), const2)],
        out_specs=pl.BlockSpec((t, wpool + wsgu), lambda i: (i, 0)),
        out_shape=jax.ShapeDtypeStruct((m, wpool + wsgu), BF16),
        scratch_shapes=[pltpu.VMEM((t, wsgu), BF16)],
        compiler_params=_params("parallel"),
        name="mix",
    )(z, z, pool_w, pool_scale.reshape(1, wpool), ln_g.reshape(1, wsgu),
      ln_b.reshape(1, wsgu), sgu_w, sb)


def _fold_qk_kernel(wq_ref, k_ref, a_ref):
    n_mem = a_ref.shape[2]
    s = lax.dot_general(wq_ref[...].astype(BF16), k_ref[...], (((1,), (1,)), ((), ())),
                        preferred_element_type=F32)
    for b in range(a_ref.shape[0]):
        a_ref[b] = s[:, b * n_mem:(b + 1) * n_mem].astype(a_ref.dtype)


def _fold_vo_kernel(v_ref, wo_ref, b_ref):
    n_mem = b_ref.shape[1]
    s = jnp.dot(v_ref[...], wo_ref[...].astype(BF16), preferred_element_type=F32)
    for b in range(b_ref.shape[0]):
        b_ref[b] = s[b * n_mem:(b + 1) * n_mem].astype(b_ref.dtype)


def _fold_qk(wq, k, *, layer, n_batch, n_mem):
    d = wq.shape[1]
    dh = d // N_XHEADS
    rb = _tile(d, FOLD_TILE)
    return pl.pallas_call(
        _fold_qk_kernel,
        grid=(N_XHEADS, d // rb),
        in_specs=[pl.BlockSpec((None, rb, dh), lambda h, r: (layer, r, h)),
                  pl.BlockSpec((n_batch * n_mem, dh), lambda h, r: (0, h))],
        out_specs=pl.BlockSpec((n_batch, rb, n_mem), lambda h, r: (0, r, h)),
        out_shape=jax.ShapeDtypeStruct((n_batch, d, N_XHEADS * n_mem), BF16),
        compiler_params=_params("parallel", "parallel"),
        name="fold_qk",
    )(wq, k)


def _fold_vo(v, wo, *, layer, n_batch, n_mem):
    d = wo.shape[2]
    dh = d // N_XHEADS
    cb = _tile(d, FOLD_TILE)
    return pl.pallas_call(
        _fold_vo_kernel,
        grid=(N_XHEADS, d // cb),
        in_specs=[pl.BlockSpec((n_batch * n_mem, dh), lambda h, c: (0, h)),
                  pl.BlockSpec((None, dh, cb), lambda h, c: (layer, h, c))],
        out_specs=pl.BlockSpec((n_batch, n_mem, cb), lambda h, c: (0, h, c)),
        out_shape=jax.ShapeDtypeStruct((n_batch, N_XHEADS * n_mem, d), BF16),
        compiler_params=_params("parallel", "parallel"),
        name="fold_vo",
    )(v, wo)


def kernel(x, mem, ln_mix, w_in, pool_w, pool_scale, sgu_ln_g, sgu_ln_b, sgu_w, sgu_b,
           w_out, ln_x, ln_mem, w_q, w_k, w_v, w_o, ln_ffn, w_up, w_down, ln_final):
    n_batch, seq, d = x.shape
    n_mem = mem.shape[1]
    depth = w_in.shape[0]
    dh = d // N_XHEADS
    x = x.reshape(n_batch * seq, d)
    memf = mem.reshape(n_batch * n_mem, d)

    wb_in = w_in[:1].astype(BF16)
    pool_w, sgu_w = pool_w.astype(BF16), sgu_w.astype(BF16)

    xb, ssq = _prep(x, ln_mix[0])
    for l in range(depth):
        z, (wb_out,) = _mm(xb, wb_in, layer=0, ssq=ssq, casts=[(w_out, l)], name="mm_in")
        x, xb, ssq, cast = _mixout(z, pool_w, pool_scale[l], sgu_ln_g[l], sgu_ln_b[l], sgu_w,
                                   sgu_b[l], wb_out, x, ln_x[l], layer=l, seq=seq,
                                   cast=(w_up, 0) if l == 0 else None)
        if l == 0:
            wb_up = cast

        mb, mssq = _prep(memf, ln_mem[l])
        k, _ = _mm(mb, w_k, layer=l, ssq=mssq, bn=MM_BN_RES, name="mm_k")
        v, _ = _mm(mb, w_v, layer=l, ssq=mssq, bn=MM_BN_RES, name="mm_v")
        a_fold = _fold_qk(w_q, k, layer=l, n_batch=n_batch, n_mem=n_mem)
        b_fold = _fold_vo(v, w_o, layer=l, n_batch=n_batch, n_mem=n_mem)
        p, _ = _mm(xb, a_fold, ssq=ssq, post_scale=dh ** -0.5, act="group_softmax",
                   n_groups=N_XHEADS, bn=N_XHEADS * n_mem, name="mm_scores")
        (x, xb, ssq), _ = _mm(p, b_fold, res=x, gain=ln_ffn[l], out_dtype=F32,
                              name="mm_attn_out")

        last = l + 1 == depth
        hid, cast = _mm(xb, wb_up, layer=0, ssq=ssq, act="relu2",
                        casts=[(w_down, l)] + ([] if last else [(w_up, l + 1)]), name="mm_up")
        wb_down = cast[0]
        if last:
            x, _ = _mm(hid, wb_down, layer=0, res=x, out_dtype=F32, bk=MM_BK_DOWN,
                       name="mm_down")
        else:
            wb_up = cast[1]
            (x, xb, ssq), (wb_in,) = _mm(hid, wb_down, layer=0, res=x, gain=ln_mix[l + 1],
                                         out_dtype=F32, casts=[(w_in, l + 1)],
                                         bk=MM_BK_DOWN, name="mm_down")

    return _final_norm(x, ln_final).reshape(n_batch, seq, d)
```

```python
import functools

import jax
import jax.numpy as jnp
from jax import lax
from jax.experimental import pallas as pl
from jax.experimental.pallas import tpu as pltpu

EPS = 1e-6
POOL_WINDOWS = (2, 4, 8, 16)
POOL_HALO = 16
CHUNK = 128
SGU_HEAD = 128
N_XHEADS = 4
LANES = 128
BF16_SUBLANES = 16

V7X_VMEM_LIMIT_BYTES = 62 * 1024 * 1024

MM_BM = 1024
MM_BN = 1024
MM_BN_RES = 512
MM_BK = 4096
MM_BK_DOWN = 2048
MM_BK_DOWN_LAST = 4096
ROW_TILE = 256
FOLD_TILE = 1024

F32 = jnp.float32
BF16 = jnp.bfloat16


def _tile(dim, pref):
    t = min(dim, pref)
    assert dim % t == 0, (dim, pref)
    return t


def _params(*sem):
    return pltpu.CompilerParams(dimension_semantics=sem,
                                vmem_limit_bytes=V7X_VMEM_LIMIT_BYTES)


def _fold_lanes(v):
    acc = v[:, :LANES]
    for c in range(1, v.shape[1] // LANES):
        acc = acc + v[:, c * LANES:(c + 1) * LANES]
    return acc


def _prep_kernel(x_ref, g_ref, xb_ref, ssq_ref):
    x = x_ref[...]
    ssq_ref[...] = _fold_lanes(x * x)
    xb_ref[...] = (x * g_ref[...]).astype(BF16)


def _prep(x, g):
    m, d = x.shape
    bm = _tile(m, ROW_TILE)
    return pl.pallas_call(
        _prep_kernel,
        grid=(m // bm,),
        in_specs=[pl.BlockSpec((bm, d), lambda i: (i, 0)),
                  pl.BlockSpec((1, d), lambda i: (0, 0))],
        out_specs=[pl.BlockSpec((bm, d), lambda i: (i, 0)),
                   pl.BlockSpec((bm, LANES), lambda i: (i, 0))],
        out_shape=[jax.ShapeDtypeStruct((m, d), BF16),
                   jax.ShapeDtypeStruct((m, LANES), F32)],
        compiler_params=_params("parallel"),
        name="prep",
    )(x, g.reshape(1, d))


def _final_norm_kernel(x_ref, g_ref, o_ref):
    x = x_ref[...]
    y = x * lax.rsqrt(jnp.mean(x * x, axis=-1, keepdims=True) + EPS)
    o_ref[...] = y * g_ref[...]


def _final_norm(x, g):
    m, d = x.shape
    bm = _tile(m, ROW_TILE)
    return pl.pallas_call(
        _final_norm_kernel,
        grid=(m // bm,),
        in_specs=[pl.BlockSpec((bm, d), lambda i: (i, 0)),
                  pl.BlockSpec((1, d), lambda i: (0, 0))],
        out_specs=pl.BlockSpec((bm, d), lambda i: (i, 0)),
        out_shape=jax.ShapeDtypeStruct((m, d), F32),
        compiler_params=_params("parallel"),
        name="final_norm",
    )(x, g.reshape(1, d))


def _mm_kernel(*refs, nk, has_ssq, has_res, has_gain, n_casts, cast_w, act, n_groups,
               post_scale, inv_k):
    refs = list(refs)
    a_ref, w_ref = refs[0], refs[1]
    pos = 2
    ssq_ref = res_ref = gain_ref = None
    if has_ssq:
        ssq_ref = refs[pos]
        pos += 1
    if has_res:
        res_ref = refs[pos]
        pos += 1
    if has_gain:
        gain_ref = refs[pos]
        pos += 1
    cast_in = refs[pos:pos + n_casts]
    pos += n_casts
    o_ref = refs[pos]
    pos += 1
    xb_ref = ssq_out_ref = None
    if has_gain:
        xb_ref, ssq_out_ref = refs[pos], refs[pos + 1]
        pos += 2
    cast_out = refs[pos:pos + n_casts]
    pos += n_casts
    acc_ref = refs[pos] if nk > 1 else None

    def cast_slabs():
        for ci, co in zip(cast_in, cast_out):
            if len(co.shape) == 2:
                co[...] = ci[...].astype(BF16)
            else:
                cb = co.shape[2]
                for c in range(co.shape[0]):
                    co[c] = ci[:, c * cb:(c + 1) * cb].astype(BF16)

    if nk == 1:
        cast_slabs()
    elif n_casts:
        pl.when(pl.program_id(2) == 0)(cast_slabs)

    def emit(new):
        xb_ref[...] = (new * gain_ref[...]).astype(BF16)
        ssq_out_ref[...] = _fold_lanes(new * new)

    def dot():
        w = w_ref[...]
        if cast_w:
            w = w.astype(BF16)
        return jnp.dot(a_ref[...], w, preferred_element_type=F32)

    if nk > 1:
        assert nk % 2 == 0 and has_res and not has_ssq and act is None and post_scale is None
        k = pl.program_id(2)

        @pl.when(k == 0)
        def _():
            acc_ref[...] = res_ref[...] + dot()

        @pl.when(jnp.logical_and(k > 0, k % 2 == 0))
        def _():
            acc_ref[...] = o_ref[...] + dot()

        @pl.when(jnp.logical_and(k % 2 == 1, k < nk - 1))
        def _():
            o_ref[...] = acc_ref[...] + dot()

        @pl.when(k == nk - 1)
        def _():
            new = acc_ref[...] + dot()
            o_ref[...] = new
            if has_gain:
                emit(new)
        return

    acc = dot()
    if has_ssq:
        acc = acc * lax.rsqrt(jnp.sum(ssq_ref[...], axis=-1, keepdims=True) * inv_k + EPS)
    if post_scale is not None:
        acc = acc * post_scale
    if act == "relu2":
        acc = jnp.square(jnp.maximum(acc, 0.0))
    elif act == "group_softmax":
        gw = acc.shape[1] // n_groups
        parts = []
        for gi in range(n_groups):
            s = acc[:, gi * gw:(gi + 1) * gw]
            e = jnp.exp(s - jnp.max(s, axis=-1, keepdims=True))
            parts.append(e / jnp.sum(e, axis=-1, keepdims=True))
        acc = jnp.concatenate(parts, axis=1)
    if has_res:
        acc = acc + res_ref[...]
    o_ref[...] = acc.astype(o_ref.dtype)
    if has_gain:
        emit(acc)


def _mm(a, w, *, layer=None, ssq=None, post_scale=None, res=None, act=None, n_groups=1,
        gain=None, casts=(), out_dtype=BF16, bm=None, bn=None, bk=None, name="mm"):
    m, kdim = a.shape
    nl, _, n = w.shape
    rows_per_w = m if layer is not None else m // nl
    bm = _tile(rows_per_w, bm or MM_BM)
    bn = _tile(n, bn or MM_BN)
    bk = _tile(kdim, bk or MM_BK)
    nk = kdim // bk
    blocks_per_w = rows_per_w // bm
    if layer is not None:
        w_map = lambda i, j, k: (layer, k, j)
    else:
        w_map = lambda i, j, k: (i // blocks_per_w, k, j)

    in_specs = [pl.BlockSpec((bm, bk), lambda i, j, k: (i, k)),
                pl.BlockSpec((None, bk, bn), w_map)]
    args = [a, w]
    if ssq is not None:
        in_specs.append(pl.BlockSpec((bm, ssq.shape[1]), lambda i, j, k: (i, 0)))
        args.append(ssq)
    if res is not None:
        in_specs.append(pl.BlockSpec((bm, bn), lambda i, j, k: (i, j)))
        args.append(res)
    out_specs = [pl.BlockSpec((bm, bn), lambda i, j, k: (i, j))]
    out_shape = [jax.ShapeDtypeStruct((m, n), out_dtype)]
    if gain is not None:
        in_specs.append(pl.BlockSpec((1, bn), lambda i, j, k: (0, j)))
        args.append(gain.reshape(1, n))
        out_specs += [pl.BlockSpec((bm, bn), lambda i, j, k: (i, j)),
                      pl.BlockSpec((bm, LANES), lambda i, j, k: (i, j))]
        out_shape += [jax.ShapeDtypeStruct((m, n), BF16),
                      jax.ShapeDtypeStruct((m, LANES * (n // bn)), F32)]
    n_main = len(out_shape)
    gi, gj = m // bm, n // bn
    for src, src_layer, *col_block in casts:
        _, kw, nw = src.shape
        cj = max(c for c in range(1, gj + 1)
                 if kw % (gi * c) == 0 and (kw // (gi * c)) % BF16_SUBLANES == 0)
        rows = kw // (gi * cj)
        slab = lambda i, j, k, cj=cj: i * cj + jnp.minimum(j, cj - 1)
        in_specs.append(pl.BlockSpec(
            (None, rows, nw), lambda i, j, k, sl=slab, l=src_layer: (l, sl(i, j, k), 0)))
        args.append(src)
        nblk = nw // col_block[0] if col_block else 1
        first = nblk if col_block else None
        out_specs.append(pl.BlockSpec(
            (first, rows, nw // nblk), lambda i, j, k, sl=slab: (0, sl(i, j, k), 0)))
        out_shape.append(jax.ShapeDtypeStruct((nblk, kw, nw // nblk), BF16))
    kern = functools.partial(
        _mm_kernel, nk=nk, has_ssq=ssq is not None, has_res=res is not None,
        has_gain=gain is not None, n_casts=len(casts), cast_w=w.dtype != BF16, act=act,
        n_groups=n_groups, post_scale=post_scale, inv_k=1.0 / kdim)
    outs = pl.pallas_call(
        kern,
        grid=(gi, gj, nk),
        in_specs=in_specs,
        out_specs=out_specs,
        out_shape=out_shape,
        scratch_shapes=[pltpu.VMEM((bm, bn), F32)] if nk > 1 else [],
        compiler_params=_params("parallel", "arbitrary", "arbitrary"),
        name=name,
    )(*args)
    main =tuple(outs[:n_main]) if gain is not None else outs[0]
    return main, list(outs[n_main:])


def _mix_rows(z_ref, halo_ref, pw_ref, ps_ref, lg_ref, lb_ref, sw_ref, sb_ref, y_ref, v_ref,
              pos0, wpool, wsgu):
    t = z_ref.shape[0]
    n_groups = len(POOL_WINDOWS)
    gc = wpool // n_groups

    halo = halo_ref[...]
    halo = jnp.where(pos0 == 0, jnp.zeros_like(halo), halo)
    row = lax.broadcasted_iota(jnp.int32, (t, t + POOL_HALO), 0)
    col = lax.broadcasted_iota(jnp.int32, (t, t + POOL_HALO), 1)
    lag = row + POOL_HALO - col
    pos = pos0 + lax.broadcasted_iota(jnp.int32, (t, 1), 0)
    for gi, win in enumerate(POOL_WINDOWS):
        cs = slice(gi * gc, (gi + 1) * gc)
        zg = z_ref[:, cs]
        band = jnp.logical_and(lag >= 0, lag < win).astype(BF16)
        wsum = jnp.dot(band, jnp.concatenate([halo[:, cs], zg], axis=0),
                       preferred_element_type=F32)
        cnt = jnp.minimum(pos + 1, win).astype(F32)
        d = wsum / cnt - zg.astype(F32)
        yg = jnp.dot(d.astype(BF16), pw_ref[gi], preferred_element_type=F32)
        y_ref[:, cs] = (yg * ps_ref[:, cs]).astype(BF16)

    gv = jax.nn.gelu(z_ref[:, wpool + wsgu:].astype(F32))
    mu = jnp.mean(gv, axis=-1, keepdims=True)
    cen = gv - mu
    var = jnp.mean(cen * cen, axis=-1, keepdims=True)
    v_ref[...] = (cen * lax.rsqrt(var + EPS) * lg_ref[...] + lb_ref[...]).astype(BF16)

    tri = (lax.broadcasted_iota(jnp.int32, (CHUNK, CHUNK), 0)
           >= lax.broadcasted_iota(jnp.int32, (CHUNK, CHUNK), 1))
    for h in range(wsgu // SGU_HEAD):
        hs = slice(h * SGU_HEAD, (h + 1) * SGU_HEAD)
        us = slice(wpool + h * SGU_HEAD, wpool + (h + 1) * SGU_HEAD)
        wm = jnp.where(tri, sw_ref[h], jnp.zeros((CHUNK, CHUNK), BF16))
        for c in range(t // CHUNK):
            rs = slice(c * CHUNK, (c + 1) * CHUNK)
            mixed = jnp.dot(wm, v_ref[rs, hs], preferred_element_type=F32) + sb_ref[:, hs]
            u = jax.nn.gelu(z_ref[rs, us].astype(F32))
            y_ref[rs, us] = (u * mixed).astype(BF16)


def _mixout_kernel(*refs, seq, wpool, wsgu, n_tiles, has_cast):
    (z_ref, halo_ref, pw_ref, ps_ref, lg_ref, lb_ref, sw_ref, sb_ref,
     w_ref, res_ref, gain_ref, y0_hbm) = refs[:12]
    pos = 12
    cast_in = cast_out = None
    if has_cast:
        cast_in = refs[pos]
        pos += 1
    o_ref, xb_ref, ssq_out_ref = refs[pos:pos + 3]
    pos += 3
    if has_cast:
        cast_out = refs[pos]
        pos += 1
    y_even, y_odd, v_ref, y0_sem = refs[pos:pos + 4]
    i, j = pl.program_id(0), pl.program_id(1)
    bm = y_even.shape[0]

    if has_cast:
        cast_out[...] = cast_in[...].astype(BF16)

    @pl.when(jnp.logical_and(i == 0, j == 0))
    def _():
        y0_copy = pltpu.make_async_copy(y0_hbm, y_even, y0_sem)
        y0_copy.start()
        y0_copy.wait()

    row0 = jnp.minimum(i + 1, n_tiles - 1) * bm + j * CHUNK

    def step(y_read, y_write):
        chunk = y_write.at[pl.ds(pl.multiple_of(j * CHUNK, CHUNK), CHUNK), :]
        _mix_rows(z_ref, halo_ref, pw_ref, ps_ref, lg_ref, lb_ref, sw_ref, sb_ref, chunk,
                  v_ref, row0 % seq, wpool, wsgu)
        new = jnp.dot(y_read[...], w_ref[...], preferred_element_type=F32) + res_ref[...]
        o_ref[...] = new
        xb_ref[...] = (new * gain_ref[...]).astype(BF16)
        ssq_out_ref[...] = _fold_lanes(new * new)

    @pl.when(i % 2 == 0)
    def _():
        step(y_even, y_odd)

    @pl.when(i % 2 == 1)
    def _():
        step(y_odd, y_even)


def _mixout_bn(seq, n):
    gj = _tile(seq, MM_BM) // CHUNK
    assert n % (gj * LANES) == 0
    return n // gj


def _mixout(z, pool_w, pool_scale, ln_g, ln_b, sgu_w, sgu_b, w, res, gain, *, layer, seq,
            cast=None):
    m, d_in = z.shape
    _, n_groups, gc, _ = pool_w.shape
    wpool = n_groups * gc
    wsgu = (d_in - wpool) // 2
    n_heads = wsgu // SGU_HEAD
    assert n_groups == len(POOL_WINDOWS) and sgu_w.shape[1:] == (n_heads, CHUNK, CHUNK)
    gj, kdim, bn = w.shape
    n = gj * bn
    bm = _tile(seq, MM_BM)
    n_tiles = m // bm
    assert kdim == wpool + wsgu and bn == _mixout_bn(seq, n)
    halo_blocks = CHUNK // POOL_HALO
    sb = jnp.repeat(jnp.transpose(sgu_b), SGU_HEAD, axis=1)

    y0 = _mix(z, pool_w, pool_scale, ln_g, ln_b, sgu_w, sgu_b, layer=layer, seq=seq, rows=bm)

    chunk_idx = lambda i, j: jnp.minimum(i + 1, n_tiles - 1) * gj + j
    const2 = lambda i, j: (0, 0)
    in_specs = [pl.BlockSpec((CHUNK, d_in), lambda i, j: (chunk_idx(i, j), 0)),
                pl.BlockSpec((POOL_HALO, wpool),
                             lambda i, j: (jnp.maximum(chunk_idx(i, j) * halo_blocks - 1, 0), 0)),
                pl.BlockSpec((None, n_groups, gc, gc), lambda i, j: (layer, 0, 0, 0)),
                pl.BlockSpec((1, wpool), const2),
                pl.BlockSpec((1, wsgu), const2),
                pl.BlockSpec((1, wsgu), const2),
                pl.BlockSpec((None, n_heads, CHUNK, CHUNK), lambda i, j: (layer, 0, 0, 0)),
                pl.BlockSpec((CHUNK, wsgu), const2),
                pl.BlockSpec((None, kdim, bn), lambda i, j: (j, 0, 0)),
                pl.BlockSpec((bm, bn), lambda i, j: (i, j)),
                pl.BlockSpec((1, bn), lambda i, j: (0, j)),
                pl.BlockSpec(memory_space=pl.ANY)]
    args = [z, z, pool_w, pool_scale.reshape(1, wpool), ln_g.reshape(1, wsgu),
            ln_b.reshape(1, wsgu), sgu_w, sb, w, res, gain.reshape(1, n), y0]
    out_specs = [pl.BlockSpec((bm, bn), lambda i, j: (i, j)),
                 pl.BlockSpec((bm, bn), lambda i, j: (i, j)),
                 pl.BlockSpec((bm, LANES), lambda i, j: (i, j))]
    out_shape = [jax.ShapeDtypeStruct((m, n), F32),
                 jax.ShapeDtypeStruct((m, n), BF16),
                 jax.ShapeDtypeStruct((m, LANES * gj), F32)]
    if cast is not None:
        src, src_layer = cast
        _, kw, nw = src.shape
        steps = n_tiles * gj
        cs = max(c for c in range(1, steps + 1)
                 if kw % c == 0 and (kw // c) % BF16_SUBLANES == 0)
        slab = lambda i, j: jnp.minimum(i * gj + j, cs - 1)
        in_specs.append(pl.BlockSpec((None, kw // cs, nw),
                                     lambda i, j: (src_layer, slab(i, j), 0)))
        args.append(src)
        out_specs.append(pl.BlockSpec((None, kw // cs, nw), lambda i, j: (0, slab(i, j), 0)))
        out_shape.append(jax.ShapeDtypeStruct((1, kw, nw), BF16))
    kern = functools.partial(_mixout_kernel, seq=seq, wpool=wpool, wsgu=wsgu, n_tiles=n_tiles,
                             has_cast=cast is not None)
    outs = pl.pallas_call(
        kern,
        grid=(n_tiles, gj),
        in_specs=in_specs,
        out_specs=out_specs,
        out_shape=out_shape,
        scratch_shapes=[pltpu.VMEM((bm, kdim), BF16), pltpu.VMEM((bm, kdim), BF16),
                        pltpu.VMEM((CHUNK, wsgu), BF16), pltpu.SemaphoreType.DMA(())],
        compiler_params=_params("arbitrary", "arbitrary"),
        name="mixout",
    )(*args)
    return outs[0], outs[1], outs[2], (outs[3] if cast is not None else None)


def _mix_kernel(z_ref, halo_ref, pw_ref, ps_ref, lg_ref, lb_ref, sw_ref, sb_ref,
                y_ref, v_ref, *, seq, wpool, wsgu):
    pos0 = (pl.program_id(0) * z_ref.shape[0]) % seq
    _mix_rows(z_ref, halo_ref, pw_ref, ps_ref, lg_ref, lb_ref, sw_ref, sb_ref, y_ref, v_ref,
              pos0, wpool, wsgu)


def _mix(z, pool_w, pool_scale, ln_g, ln_b, sgu_w, sgu_b, *, layer, seq, rows):
    m, d_in = rows, z.shape[1]
    _, n_groups, gc, _ = pool_w.shape
    wpool = n_groups * gc
    wsgu = (d_in - wpool) // 2
    n_heads = wsgu // SGU_HEAD
    assert n_groups == len(POOL_WINDOWS) and sgu_w.shape[1:] == (n_heads, CHUNK, CHUNK)
    t = _tile(seq, ROW_TILE)
    assert t % CHUNK == 0 and t % POOL_HALO == 0
    halo_blocks = t // POOL_HALO
    sb = jnp.repeat(jnp.transpose(sgu_b), SGU_HEAD, axis=1)
    kern = functools.partial(_mix_kernel, seq=seq, wpool=wpool, wsgu=wsgu)
    const2 = lambda i: (0, 0)
    return pl.pallas_call(
        kern,
        grid=(m // t,),
        in_specs=[pl.BlockSpec((t, d_in), lambda i: (i, 0)),
                  pl.BlockSpec((POOL_HALO, wpool),
                               lambda i: (jnp.maximum(i * halo_blocks - 1, 0), 0)),
                  pl.BlockSpec((None, n_groups, gc, gc), lambda i: (layer, 0, 0, 0)),
                  pl.BlockSpec((1, wpool), const2),
                  pl.BlockSpec((1, wsgu), const2),
                  pl.BlockSpec((1, wsgu), const2),
                  pl.BlockSpec((None, n_heads, CHUNK, CHUNK), lambda i: (layer, 0, 0, 0)),
                  pl.BlockSpec((CHUNK, wsgu), const2)],
        out_specs=pl.BlockSpec((t, wpool + wsgu), lambda i: (i, 0)),
        out_shape=jax.ShapeDtypeStruct((m, wpool + wsgu), BF16),
        scratch_shapes=[pltpu.VMEM((t, wsgu), BF16)],
        compiler_params=_params("parallel"),
        name="mix",
    )(z, z, pool_w, pool_scale.reshape(1, wpool), ln_g.reshape(1, wsgu),
      ln_b.reshape(1, wsgu), sgu_w, sb)


def _fold_qk_kernel(wq_ref, k_ref, a_ref):
    n_mem = a_ref.shape[2]
    s = lax.dot_general(wq_ref[...].astype(BF16), k_ref[...], (((1,), (1,)), ((), ())),
                        preferred_element_type=F32)
    for b in range(a_ref.shape[0]):
        a_ref[b] = s[:, b * n_mem:(b + 1) * n_mem].astype(a_ref.dtype)


def _fold_vo_kernel(v_ref, wo_ref, b_ref):
    n_mem = b_ref.shape[1]
    s = jnp.dot(v_ref[...], wo_ref[...].astype(BF16), preferred_element_type=F32)
    for b in range(b_ref.shape[0]):
        b_ref[b] = s[b * n_mem:(b + 1) * n_mem].astype(b_ref.dtype)


def _fold_qk(wq, k, *, layer, n_batch, n_mem):
    d = wq.shape[1]
    dh = d // N_XHEADS
    rb = _tile(d, FOLD_TILE)
    return pl.pallas_call(
        _fold_qk_kernel,
        grid=(N_XHEADS, d // rb),
        in_specs=[pl.BlockSpec((None, rb, dh), lambda h, r: (layer, r, h)),
                  pl.BlockSpec((n_batch * n_mem, dh), lambda h, r: (0, h))],
        out_specs=pl.BlockSpec((n_batch, rb, n_mem), lambda h, r: (0, r, h)),
        out_shape=jax.ShapeDtypeStruct((n_batch, d, N_XHEADS * n_mem), BF16),
        compiler_params=_params("parallel", "parallel"),
        name="fold_qk",
    )(wq, k)


def _fold_vo(v, wo, *, layer, n_batch, n_mem):
    d = wo.shape[2]
    dh = d // N_XHEADS
    cb = _tile(d, FOLD_TILE)
    return pl.pallas_call(
        _fold_vo_kernel,
        grid=(N_XHEADS, d // cb),
        in_specs=[pl.BlockSpec((n_batch * n_mem, dh), lambda h, c: (0, h)),
                  pl.BlockSpec((None, dh, cb), lambda h, c: (layer, h, c))],
        out_specs=pl.BlockSpec((n_batch, n_mem, cb), lambda h, c: (0, h, c)),
        out_shape=jax.ShapeDtypeStruct((n_batch, N_XHEADS * n_mem, d), BF16),
        compiler_params=_params("parallel", "parallel"),
        name="fold_vo",
    )(v, wo)


def kernel(x, mem, ln_mix, w_in, pool_w, pool_scale, sgu_ln_g, sgu_ln_b, sgu_w, sgu_b,
           w_out, ln_x, ln_mem, w_q, w_k, w_v, w_o, ln_ffn, w_up, w_down, ln_final):
    n_batch, seq, d = x.shape
    n_mem = mem.shape[1]
    depth = w_in.shape[0]
    dh = d // N_XHEADS
    x = x.reshape(n_batch * seq, d)
    memf = mem.reshape(n_batch * n_mem, d)

    wb_in = w_in[:1].astype(BF16)
    pool_w, sgu_w = pool_w.astype(BF16), sgu_w.astype(BF16)

    xb, ssq = _prep(x, ln_mix[0])
    for l in range(depth):
        z, (wb_out,) = _mm(xb, wb_in, layer=0, ssq=ssq,
                           casts=[(w_out, l, _mixout_bn(seq, d))], name="mm_in")
        x, xb, ssq, cast = _mixout(z, pool_w, pool_scale[l], sgu_ln_g[l], sgu_ln_b[l], sgu_w,
                                   sgu_b[l], wb_out, x, ln_x[l], layer=l, seq=seq,
                                   cast=(w_up, 0) if l == 0 else None)
        if l == 0:
            wb_up = cast

        mb, mssq = _prep(memf, ln_mem[l])
        k, _ = _mm(mb, w_k, layer=l, ssq=mssq, bn=MM_BN_RES, name="mm_k")
        v, _ = _mm(mb, w_v, layer=l, ssq=mssq, bn=MM_BN_RES, name="mm_v")
        a_fold = _fold_qk(w_q, k, layer=l, n_batch=n_batch, n_mem=n_mem)
        b_fold = _fold_vo(v, w_o, layer=l, n_batch=n_batch, n_mem=n_mem)
        p, _ = _mm(xb, a_fold, ssq=ssq, post_scale=dh ** -0.5, act="group_softmax",
                   n_groups=N_XHEADS, bn=N_XHEADS * n_mem, name="mm_scores")
        (x, xb, ssq), _ = _mm(p, b_fold, res=x, gain=ln_ffn[l], out_dtype=F32,
                              name="mm_attn_out")

        last = l + 1 == depth
        hid, cast = _mm(xb, wb_up, layer=0, ssq=ssq, act="relu2",
                        casts=[(w_down, l)] + ([] if last else [(w_up, l + 1)]), name="mm_up")
        wb_down = cast[0]
        if last:
            x, _ = _mm(hid, wb_down, layer=0, res=x, out_dtype=F32, bk=MM_BK_DOWN_LAST,
                       name="mm_down")
        else:
            wb_up = cast[1]
            (x, xb, ssq), (wb_in,) = _mm(hid, wb_down, layer=0, res=x, gain=ln_mix[l + 1],
                                         out_dtype=F32, casts=[(w_in, l + 1)],
                                         bk=MM_BK_DOWN, name="mm_down")

    return _final_norm(x, ln_final).reshape(n_batch, seq, d)
```

```python
import functools

import jax
import jax.numpy as jnp
from jax import lax
from jax.experimental import pallas as pl
from jax.experimental.pallas import tpu as pltpu

EPS = 1e-6
POOL_WINDOWS = (2, 4, 8, 16)
POOL_HALO = 16
CHUNK = 128
SGU_HEAD = 128
N_XHEADS = 4
LANES = 128
BF16_SUBLANES = 16

V7X_VMEM_LIMIT_BYTES = 62 * 1024 * 1024

MM_BM = 1024
MM_BN = 1024
MM_BN_RES = 512
MM_BK = 4096
MM_BK_DOWN = 2048
MM_BK_DOWN_LAST = 4096
ATTN_OUT_BM = 512
ROW_TILE = 256
FOLD_TILE = 1024

F32 = jnp.float32
BF16 = jnp.bfloat16


def _tile(dim, pref):
    t = min(dim, pref)
    assert dim % t == 0, (dim, pref)
    return t


def _params(*sem):
    return pltpu.CompilerParams(dimension_semantics=sem,
                                vmem_limit_bytes=V7X_VMEM_LIMIT_BYTES)


def _fold_lanes(v):
    acc = v[:, :LANES]
    for c in range(1, v.shape[1] // LANES):
        acc = acc + v[:, c * LANES:(c + 1) * LANES]
    return acc


def _prep_kernel(x_ref, g_ref, xb_ref, ssq_ref):
    x = x_ref[...]
    ssq_ref[...] = _fold_lanes(x * x)
    xb_ref[...] = (x * g_ref[...]).astype(BF16)


def _prep(x, g):
    m, d = x.shape
    bm = _tile(m, ROW_TILE)
    return pl.pallas_call(
        _prep_kernel,
        grid=(m // bm,),
        in_specs=[pl.BlockSpec((bm, d), lambda i: (i, 0)),
                  pl.BlockSpec((1, d), lambda i: (0, 0))],
        out_specs=[pl.BlockSpec((bm, d), lambda i: (i, 0)),
                   pl.BlockSpec((bm, LANES), lambda i: (i, 0))],
        out_shape=[jax.ShapeDtypeStruct((m, d), BF16),
                   jax.ShapeDtypeStruct((m, LANES), F32)],
        compiler_params=_params("parallel"),
        name="prep",
    )(x, g.reshape(1, d))


def _final_norm_kernel(x_ref, g_ref, o_ref):
    x = x_ref[...]
    y = x * lax.rsqrt(jnp.mean(x * x, axis=-1, keepdims=True) + EPS)
    o_ref[...] = y * g_ref[...]


def _final_norm(x, g):
    m, d = x.shape
    bm = _tile(m, ROW_TILE)
    return pl.pallas_call(
        _final_norm_kernel,
        grid=(m // bm,),
        in_specs=[pl.BlockSpec((bm, d), lambda i: (i, 0)),
                  pl.BlockSpec((1, d), lambda i: (0, 0))],
        out_specs=pl.BlockSpec((bm, d), lambda i: (i, 0)),
        out_shape=jax.ShapeDtypeStruct((m, d), F32),
        compiler_params=_params("parallel"),
        name="final_norm",
    )(x, g.reshape(1, d))


def _mm_kernel(*refs, nk, has_ssq, has_res, has_gain, n_casts, cast_w, act, n_groups,
               post_scale, inv_k):
    refs = list(refs)
    a_ref, w_ref = refs[0], refs[1]
    pos = 2
    ssq_ref = res_ref = gain_ref = None
    if has_ssq:
        ssq_ref = refs[pos]
        pos += 1
    if has_res:
        res_ref = refs[pos]
        pos += 1
    if has_gain:
        gain_ref = refs[pos]
        pos += 1
    cast_in = refs[pos:pos + n_casts]
    pos += n_casts
    o_ref = refs[pos]
    pos += 1
    xb_ref = ssq_out_ref = None
    if has_gain:
        xb_ref, ssq_out_ref = refs[pos], refs[pos + 1]
        pos += 2
    cast_out = refs[pos:pos + n_casts]
    pos += n_casts
    acc_ref = refs[pos] if nk > 1 else None

    def cast_slabs():
        for ci, co in zip(cast_in, cast_out):
            if len(co.shape) == 2:
                co[...] = ci[...].astype(BF16)
            else:
                cb = co.shape[2]
                for c in range(co.shape[0]):
                    co[c] = ci[:, c * cb:(c + 1) * cb].astype(BF16)

    if nk == 1:
        cast_slabs()
    elif n_casts:
        pl.when(pl.program_id(2) == 0)(cast_slabs)

    def emit(new):
        xb_ref[...] = (new * gain_ref[...]).astype(BF16)
        ssq_out_ref[...] = _fold_lanes(new * new)

    def dot():
        w = w_ref[...]
        if cast_w:
            w = w.astype(BF16)
        return jnp.dot(a_ref[...], w, preferred_element_type=F32)

    if nk > 1:
        assert nk % 2 == 0 and has_res and not has_ssq and act is None and post_scale is None
        k = pl.program_id(2)

        @pl.when(k == 0)
        def _():
            acc_ref[...] = res_ref[...] + dot()

        @pl.when(jnp.logical_and(k > 0, k % 2 == 0))
        def _():
            acc_ref[...] = o_ref[...] + dot()

        @pl.when(jnp.logical_and(k % 2 == 1, k < nk - 1))
        def _():
            o_ref[...] = acc_ref[...] + dot()

        @pl.when(k == nk - 1)
        def _():
            new = acc_ref[...] + dot()
            o_ref[...] = new
            if has_gain:
                emit(new)
        return

    acc = dot()
    if has_ssq:
        acc = acc * lax.rsqrt(jnp.sum(ssq_ref[...], axis=-1, keepdims=True) * inv_k + EPS)
    if post_scale is not None:
        acc = acc * post_scale
    if act == "relu2":
        acc = jnp.square(jnp.maximum(acc, 0.0))
    elif act == "group_softmax":
        gw = acc.shape[1] // n_groups
        parts = []
        for gi in range(n_groups):
            s = acc[:, gi * gw:(gi + 1) * gw]
            e = jnp.exp(s - jnp.max(s, axis=-1, keepdims=True))
            parts.append(e / jnp.sum(e, axis=-1, keepdims=True))
        acc = jnp.concatenate(parts, axis=1)
    if has_res:
        acc = acc + res_ref[...]
    o_ref[...] = acc.astype(o_ref.dtype)
    if has_gain:
        emit(acc)


def _mm(a, w, *, layer=None, ssq=None, post_scale=None, res=None, act=None, n_groups=1,
        gain=None, casts=(), out_dtype=BF16, bm=None, bn=None, bk=None, name="mm"):
    m, kdim = a.shape
    nl, _, n = w.shape
    rows_per_w = m if layer is not None else m // nl
    bm = _tile(rows_per_w, bm or MM_BM)
    bn = _tile(n, bn or MM_BN)
    bk = _tile(kdim, bk or MM_BK)
    nk = kdim // bk
    blocks_per_w = rows_per_w // bm
    if layer is not None:
        w_map = lambda i, j, k: (layer, k, j)
    else:
        w_map = lambda i, j, k: (i // blocks_per_w, k, j)

    in_specs = [pl.BlockSpec((bm, bk), lambda i, j, k: (i, k)),
                pl.BlockSpec((None, bk, bn), w_map)]
    args = [a, w]
    if ssq is not None:
        in_specs.append(pl.BlockSpec((bm, ssq.shape[1]), lambda i, j, k: (i, 0)))
        args.append(ssq)
    if res is not None:
        in_specs.append(pl.BlockSpec((bm, bn), lambda i, j, k: (i, j)))
        args.append(res)
    out_specs = [pl.BlockSpec((bm, bn), lambda i, j, k: (i, j))]
    out_shape = [jax.ShapeDtypeStruct((m, n), out_dtype)]
    if gain is not None:
        in_specs.append(pl.BlockSpec((1, bn), lambda i, j, k: (0, j)))
        args.append(gain.reshape(1, n))
        out_specs += [pl.BlockSpec((bm, bn), lambda i, j, k: (i, j)),
                      pl.BlockSpec((bm, LANES), lambda i, j, k: (i, j))]
        out_shape += [jax.ShapeDtypeStruct((m, n), BF16),
                      jax.ShapeDtypeStruct((m, LANES * (n // bn)), F32)]
    n_main = len(out_shape)
    gi, gj = m // bm, n // bn
    for src, src_layer, *col_block in casts:
        _, kw, nw = src.shape
        cj = max(c for c in range(1, gj + 1)
                 if kw % (gi * c) == 0 and (kw // (gi * c)) % BF16_SUBLANES == 0)
        rows = kw // (gi * cj)
        slab = lambda i, j, k, cj=cj: i * cj + jnp.minimum(j, cj - 1)
        in_specs.append(pl.BlockSpec(
            (None, rows, nw), lambda i, j, k, sl=slab, l=src_layer: (l, sl(i, j, k), 0)))
        args.append(src)
        nblk = nw // col_block[0] if col_block else 1
        first = nblk if col_block else None
        out_specs.append(pl.BlockSpec(
            (first, rows, nw // nblk), lambda i, j, k, sl=slab: (0, sl(i, j, k), 0)))
        out_shape.append(jax.ShapeDtypeStruct((nblk, kw, nw // nblk), BF16))
    kern = functools.partial(
        _mm_kernel, nk=nk, has_ssq=ssq is not None, has_res=res is not None,
        has_gain=gain is not None, n_casts=len(casts), cast_w=w.dtype != BF16, act=act,
        n_groups=n_groups, post_scale=post_scale, inv_k=1.0 / kdim)
    outs = pl.pallas_call(
        kern,
        grid=(gi, gj, nk),
        in_specs=in_specs,
        out_specs=out_specs,
        out_shape=out_shape,
        scratch_shapes=[pltpu.VMEM((bm, bn), F32)] if nk > 1 else [],
        compiler_params=_params("parallel", "arbitrary", "arbitrary"),
        name=name,
    )(*args)
    main =tuple(outs[:n_main]) if gain is not None else outs[0]
    return main, list(outs[n_main:])


def _mix_rows(z_ref, halo_ref, pw_ref, ps_ref, lg_ref, lb_ref, sw_ref, sb_ref, y_ref, v_ref,
              pos0, wpool, wsgu):
    t = z_ref.shape[0]
    n_groups = len(POOL_WINDOWS)
    gc = wpool // n_groups

    halo = halo_ref[...]
    halo = jnp.where(pos0 == 0, jnp.zeros_like(halo), halo)
    row = lax.broadcasted_iota(jnp.int32, (t, t + POOL_HALO), 0)
    col = lax.broadcasted_iota(jnp.int32, (t, t + POOL_HALO), 1)
    lag = row + POOL_HALO - col
    pos = pos0 + lax.broadcasted_iota(jnp.int32, (t, 1), 0)
    for gi, win in enumerate(POOL_WINDOWS):
        cs = slice(gi * gc, (gi + 1) * gc)
        zg = z_ref[:, cs]
        band = jnp.logical_and(lag >= 0, lag < win).astype(BF16)
        wsum = jnp.dot(band, jnp.concatenate([halo[:, cs], zg], axis=0),
                       preferred_element_type=F32)
        cnt = jnp.minimum(pos + 1, win).astype(F32)
        d = wsum / cnt - zg.astype(F32)
        yg = jnp.dot(d.astype(BF16), pw_ref[gi], preferred_element_type=F32)
        y_ref[:, cs] = (yg * ps_ref[:, cs]).astype(BF16)

    gv = jax.nn.gelu(z_ref[:, wpool + wsgu:].astype(F32))
    mu = jnp.mean(gv, axis=-1, keepdims=True)
    cen = gv - mu
    var = jnp.mean(cen * cen, axis=-1, keepdims=True)
    v_ref[...] = (cen * lax.rsqrt(var + EPS) * lg_ref[...] + lb_ref[...]).astype(BF16)

    tri = (lax.broadcasted_iota(jnp.int32, (CHUNK, CHUNK), 0)
           >= lax.broadcasted_iota(jnp.int32, (CHUNK, CHUNK), 1))
    for h in range(wsgu // SGU_HEAD):
        hs = slice(h * SGU_HEAD, (h + 1) * SGU_HEAD)
        us = slice(wpool + h * SGU_HEAD, wpool + (h + 1) * SGU_HEAD)
        wm = jnp.where(tri, sw_ref[h], jnp.zeros((CHUNK, CHUNK), BF16))
        for c in range(t // CHUNK):
            rs = slice(c * CHUNK, (c + 1) * CHUNK)
            mixed = jnp.dot(wm, v_ref[rs, hs], preferred_element_type=F32) + sb_ref[:, hs]
            u = jax.nn.gelu(z_ref[rs, us].astype(F32))
            y_ref[rs, us] = (u * mixed).astype(BF16)


def _mixout_kernel(*refs, seq, wpool, wsgu, n_tiles, has_cast):
    (z_ref, halo_ref, pw_ref, ps_ref, lg_ref, lb_ref, sw_ref, sb_ref,
     w_ref, res_ref, gain_ref, y0_hbm) = refs[:12]
    pos = 12
    cast_in = cast_out = None
    if has_cast:
        cast_in = refs[pos]
        pos += 1
    o_ref, xb_ref, ssq_out_ref = refs[pos:pos + 3]
    pos += 3
    if has_cast:
        cast_out = refs[pos]
        pos += 1
    y_even, y_odd, v_ref, y0_sem = refs[pos:pos + 4]
    i, j = pl.program_id(0), pl.program_id(1)
    bm = y_even.shape[0]

    if has_cast:
        cast_out[...] = cast_in[...].astype(BF16)

    @pl.when(jnp.logical_and(i == 0, j == 0))
    def _():
        y0_copy = pltpu.make_async_copy(y0_hbm, y_even, y0_sem)
        y0_copy.start()
        y0_copy.wait()

    row0 = jnp.minimum(i + 1, n_tiles - 1) * bm + j * CHUNK

    def step(y_read, y_write):
        chunk = y_write.at[pl.ds(pl.multiple_of(j * CHUNK, CHUNK), CHUNK), :]
        _mix_rows(z_ref, halo_ref, pw_ref, ps_ref, lg_ref, lb_ref, sw_ref, sb_ref, chunk,
                  v_ref, row0 % seq, wpool, wsgu)
        new = jnp.dot(y_read[...], w_ref[...], preferred_element_type=F32) + res_ref[...]
        o_ref[...] = new
        xb_ref[...] = (new * gain_ref[...]).astype(BF16)
        ssq_out_ref[...] = _fold_lanes(new * new)

    @pl.when(i % 2 == 0)
    def _():
        step(y_even, y_odd)

    @pl.when(i % 2 == 1)
    def _():
        step(y_odd, y_even)


def _mixout_bn(seq, n):
    gj = _tile(seq, MM_BM) // CHUNK
    assert n % (gj * LANES) == 0
    return n // gj


def _mixout(z, pool_w, pool_scale, ln_g, ln_b, sgu_w, sgu_b, w, res, gain, *, layer, seq,
            cast=None):
    m, d_in = z.shape
    _, n_groups, gc, _ = pool_w.shape
    wpool = n_groups * gc
    wsgu = (d_in - wpool) // 2
    n_heads = wsgu // SGU_HEAD
    assert n_groups == len(POOL_WINDOWS) and sgu_w.shape[1:] == (n_heads, CHUNK, CHUNK)
    gj, kdim, bn = w.shape
    n = gj * bn
    bm = _tile(seq, MM_BM)
    n_tiles = m // bm
    assert kdim == wpool + wsgu and bn == _mixout_bn(seq, n)
    halo_blocks = CHUNK // POOL_HALO
    sb = jnp.repeat(jnp.transpose(sgu_b), SGU_HEAD, axis=1)

    y0 = _mix(z, pool_w, pool_scale, ln_g, ln_b, sgu_w, sgu_b, layer=layer, seq=seq, rows=bm)

    chunk_idx = lambda i, j: jnp.minimum(i + 1, n_tiles - 1) * gj + j
    const2 = lambda i, j: (0, 0)
    in_specs = [pl.BlockSpec((CHUNK, d_in), lambda i, j: (chunk_idx(i, j), 0)),
                pl.BlockSpec((POOL_HALO, wpool),
                             lambda i, j: (jnp.maximum(chunk_idx(i, j) * halo_blocks - 1, 0), 0)),
                pl.BlockSpec((None, n_groups, gc, gc), lambda i, j: (layer, 0, 0, 0)),
                pl.BlockSpec((1, wpool), const2),
                pl.BlockSpec((1, wsgu), const2),
                pl.BlockSpec((1, wsgu), const2),
                pl.BlockSpec((None, n_heads, CHUNK, CHUNK), lambda i, j: (layer, 0, 0, 0)),
                pl.BlockSpec((CHUNK, wsgu), const2),
                pl.BlockSpec((None, kdim, bn), lambda i, j: (j, 0, 0)),
                pl.BlockSpec((bm, bn), lambda i, j: (i, j)),
                pl.BlockSpec((1, bn), lambda i, j: (0, j)),
                pl.BlockSpec(memory_space=pl.ANY)]
    args = [z, z, pool_w, pool_scale.reshape(1, wpool), ln_g.reshape(1, wsgu),
            ln_b.reshape(1, wsgu), sgu_w, sb, w, res, gain.reshape(1, n), y0]
    out_specs = [pl.BlockSpec((bm, bn), lambda i, j: (i, j)),
                 pl.BlockSpec((bm, bn), lambda i, j: (i, j)),
                 pl.BlockSpec((bm, LANES), lambda i, j: (i, j))]
    out_shape = [jax.ShapeDtypeStruct((m, n), F32),
                 jax.ShapeDtypeStruct((m, n), BF16),
                 jax.ShapeDtypeStruct((m, LANES * gj), F32)]
    if cast is not None:
        src, src_layer = cast
        _, kw, nw = src.shape
        steps = n_tiles * gj
        cs = max(c for c in range(1, steps + 1)
                 if kw % c == 0 and (kw // c) % BF16_SUBLANES == 0)
        slab = lambda i, j: jnp.minimum(i * gj + j, cs - 1)
        in_specs.append(pl.BlockSpec((None, kw // cs, nw),
                                     lambda i, j: (src_layer, slab(i, j), 0)))
        args.append(src)
        out_specs.append(pl.BlockSpec((None, kw // cs, nw), lambda i, j: (0, slab(i, j), 0)))
        out_shape.append(jax.ShapeDtypeStruct((1, kw, nw), BF16))
    kern = functools.partial(_mixout_kernel, seq=seq, wpool=wpool, wsgu=wsgu, n_tiles=n_tiles,
                             has_cast=cast is not None)
    outs = pl.pallas_call(
        kern,
        grid=(n_tiles, gj),
        in_specs=in_specs,
        out_specs=out_specs,
        out_shape=out_shape,
        scratch_shapes=[pltpu.VMEM((bm, kdim), BF16), pltpu.VMEM((bm, kdim), BF16),
                        pltpu.VMEM((CHUNK, wsgu), BF16), pltpu.SemaphoreType.DMA(())],
        compiler_params=_params("arbitrary", "arbitrary"),
        name="mixout",
    )(*args)
    return outs[0], outs[1], outs[2], (outs[3] if cast is not None else None)


def _mix_kernel(z_ref, halo_ref, pw_ref, ps_ref, lg_ref, lb_ref, sw_ref, sb_ref,
                y_ref, v_ref, *, seq, wpool, wsgu):
    pos0 = (pl.program_id(0) * z_ref.shape[0]) % seq
    _mix_rows(z_ref, halo_ref, pw_ref, ps_ref, lg_ref, lb_ref, sw_ref, sb_ref, y_ref, v_ref,
              pos0, wpool, wsgu)


def _mix(z, pool_w, pool_scale, ln_g, ln_b, sgu_w, sgu_b, *, layer, seq, rows):
    m, d_in = rows, z.shape[1]
    _, n_groups, gc, _ = pool_w.shape
    wpool = n_groups * gc
    wsgu = (d_in - wpool) // 2
    n_heads = wsgu // SGU_HEAD
    assert n_groups == len(POOL_WINDOWS) and sgu_w.shape[1:] == (n_heads, CHUNK, CHUNK)
    t = _tile(seq, ROW_TILE)
    assert t % CHUNK == 0 and t % POOL_HALO == 0
    halo_blocks = t // POOL_HALO
    sb = jnp.repeat(jnp.transpose(sgu_b), SGU_HEAD, axis=1)
    kern = functools.partial(_mix_kernel, seq=seq, wpool=wpool, wsgu=wsgu)
    const2 = lambda i: (0, 0)
    return pl.pallas_call(
        kern,
        grid=(m // t,),
        in_specs=[pl.BlockSpec((t, d_in), lambda i: (i, 0)),
                  pl.BlockSpec((POOL_HALO, wpool),
                               lambda i: (jnp.maximum(i * halo_blocks - 1, 0), 0)),
                  pl.BlockSpec((None, n_groups, gc, gc), lambda i: (layer, 0, 0, 0)),
                  pl.BlockSpec((1, wpool), const2),
                  pl.BlockSpec((1, wsgu), const2),
                  pl.BlockSpec((1, wsgu), const2),
                  pl.BlockSpec((None, n_heads, CHUNK, CHUNK), lambda i: (layer, 0, 0, 0)),
                  pl.BlockSpec((CHUNK, wsgu), const2)],
        out_specs=pl.BlockSpec((t, wpool + wsgu), lambda i: (i, 0)),
        out_shape=jax.ShapeDtypeStruct((m, wpool + wsgu), BF16),
        scratch_shapes=[pltpu.VMEM((t, wsgu), BF16)],
        compiler_params=_params("parallel"),
        name="mix",
    )(z, z, pool_w, pool_scale.reshape(1, wpool), ln_g.reshape(1, wsgu),
      ln_b.reshape(1, wsgu), sgu_w, sb)


def _fold_qk_kernel(wq_ref, k_ref, a_ref):
    n_mem = a_ref.shape[2]
    s = lax.dot_general(wq_ref[...].astype(BF16), k_ref[...], (((1,), (1,)), ((), ())),
                        preferred_element_type=F32)
    for b in range(a_ref.shape[0]):
        a_ref[b] = s[:, b * n_mem:(b + 1) * n_mem].astype(a_ref.dtype)


def _fold_vo_kernel(v_ref, wo_ref, b_ref):
    n_mem = b_ref.shape[1]
    s = jnp.dot(v_ref[...], wo_ref[...].astype(BF16), preferred_element_type=F32)
    for b in range(b_ref.shape[0]):
        b_ref[b] = s[b * n_mem:(b + 1) * n_mem].astype(b_ref.dtype)


def _fold_qk(wq, k, *, layer, n_batch, n_mem):
    d = wq.shape[1]
    dh = d // N_XHEADS
    rb = _tile(d, FOLD_TILE)
    return pl.pallas_call(
        _fold_qk_kernel,
        grid=(N_XHEADS, d // rb),
        in_specs=[pl.BlockSpec((None, rb, dh), lambda h, r: (layer, r, h)),
                  pl.BlockSpec((n_batch * n_mem, dh), lambda h, r: (0, h))],
        out_specs=pl.BlockSpec((n_batch, rb, n_mem), lambda h, r: (0, r, h)),
        out_shape=jax.ShapeDtypeStruct((n_batch, d, N_XHEADS * n_mem), BF16),
        compiler_params=_params("parallel", "parallel"),
        name="fold_qk",
    )(wq, k)


def _fold_vo(v, wo, *, layer, n_batch, n_mem):
    d = wo.shape[2]
    dh = d // N_XHEADS
    cb = _tile(d, FOLD_TILE)
    return pl.pallas_call(
        _fold_vo_kernel,
        grid=(N_XHEADS, d // cb),
        in_specs=[pl.BlockSpec((n_batch * n_mem, dh), lambda h, c: (0, h)),
                  pl.BlockSpec((None, dh, cb), lambda h, c: (layer, h, c))],
        out_specs=pl.BlockSpec((n_batch, n_mem, cb), lambda h, c: (0, h, c)),
        out_shape=jax.ShapeDtypeStruct((n_batch, N_XHEADS * n_mem, d), BF16),
        compiler_params=_params("parallel", "parallel"),
        name="fold_vo",
    )(v, wo)


def kernel(x, mem, ln_mix, w_in, pool_w, pool_scale, sgu_ln_g, sgu_ln_b, sgu_w, sgu_b,
           w_out, ln_x, ln_mem, w_q, w_k, w_v, w_o, ln_ffn, w_up, w_down, ln_final):
    n_batch, seq, d = x.shape
    n_mem = mem.shape[1]
    depth = w_in.shape[0]
    dh = d // N_XHEADS
    x = x.reshape(n_batch * seq, d)
    memf = mem.reshape(n_batch * n_mem, d)

    wb_in = w_in[:1].astype(BF16)
    pool_w, sgu_w = pool_w.astype(BF16), sgu_w.astype(BF16)

    xb, ssq = _prep(x, ln_mix[0])
    for l in range(depth):
        z, (wb_out,) = _mm(xb, wb_in, layer=0, ssq=ssq,
                           casts=[(w_out, l, _mixout_bn(seq, d))], name="mm_in")
        x, xb, ssq, cast = _mixout(z, pool_w, pool_scale[l], sgu_ln_g[l], sgu_ln_b[l], sgu_w,
                                   sgu_b[l], wb_out, x, ln_x[l], layer=l, seq=seq,
                                   cast=(w_up, 0) if l == 0 else None)
        if l == 0:
            wb_up = cast

        mb, mssq = _prep(memf, ln_mem[l])
        k, _ = _mm(mb, w_k, layer=l, ssq=mssq, bn=MM_BN_RES, name="mm_k")
        v, _ = _mm(mb, w_v, layer=l, ssq=mssq, bn=MM_BN_RES, name="mm_v")
        a_fold = _fold_qk(w_q, k, layer=l, n_batch=n_batch, n_mem=n_mem)
        b_fold = _fold_vo(v, w_o, layer=l, n_batch=n_batch, n_mem=n_mem)
        p, _ = _mm(xb, a_fold, ssq=ssq, post_scale=dh ** -0.5, act="group_softmax",
                   n_groups=N_XHEADS, bn=N_XHEADS * n_mem, name="mm_scores")
        (x, xb, ssq), _ = _mm(p, b_fold, res=x, gain=ln_ffn[l], out_dtype=F32,
                              bm=ATTN_OUT_BM, bn=d, name="mm_attn_out")

        last = l + 1 == depth
        hid, cast = _mm(xb, wb_up, layer=0, ssq=ssq, act="relu2",
                        casts=[(w_down, l)] + ([] if last else [(w_up, l + 1), (w_in, l + 1)]),
                        name="mm_up")
        wb_down = cast[0]
        if last:
            x, _ = _mm(hid, wb_down, layer=0, res=x, out_dtype=F32, bk=MM_BK_DOWN_LAST,
                       name="mm_down")
        else:
            wb_up, wb_in = cast[1:]
            (x, xb, ssq), _ = _mm(hid, wb_down, layer=0, res=x, gain=ln_mix[l + 1],
                                  out_dtype=F32, bk=MM_BK_DOWN, name="mm_down")

    return _final_norm(x, ln_final).reshape(n_batch, seq, d)
```

```python
import functools

import jax
import jax.numpy as jnp
from jax import lax
from jax.experimental import pallas as pl
from jax.experimental.pallas import tpu as pltpu

EPS = 1e-6
POOL_WINDOWS = (2, 4, 8, 16)
POOL_HALO = 16
CHUNK = 128
SGU_HEAD = 128
N_XHEADS = 4
LANES = 128
BF16_SUBLANES = 16

V7X_VMEM_LIMIT_BYTES = 62 * 1024 * 1024

MM_BM = 1024
MM_BN = 1024
MM_BN_RES = 512
MM_BK = 4096
MM_BK_DOWN = 2048
MM_BK_DOWN_LAST = 4096
ATTN_OUT_BM = 512
MIXOUT_ROWS = 128
ROW_TILE = 256
FOLD_TILE = 1024

F32 = jnp.float32
BF16 = jnp.bfloat16


def _tile(dim, pref):
    t = min(dim, pref)
    assert dim % t == 0, (dim, pref)
    return t


def _params(*sem):
    return pltpu.CompilerParams(dimension_semantics=sem,
                                vmem_limit_bytes=V7X_VMEM_LIMIT_BYTES)


def _fold_lanes(v):
    acc = v[:, :LANES]
    for c in range(1, v.shape[1] // LANES):
        acc = acc + v[:, c * LANES:(c + 1) * LANES]
    return acc


def _prep_kernel(x_ref, g_ref, xb_ref, ssq_ref):
    x = x_ref[...]
    ssq_ref[...] = _fold_lanes(x * x)
    xb_ref[...] = (x * g_ref[...]).astype(BF16)


def _prep(x, g):
    m, d = x.shape
    bm = _tile(m, ROW_TILE)
    return pl.pallas_call(
        _prep_kernel,
        grid=(m // bm,),
        in_specs=[pl.BlockSpec((bm, d), lambda i: (i, 0)),
                  pl.BlockSpec((1, d), lambda i: (0, 0))],
        out_specs=[pl.BlockSpec((bm, d), lambda i: (i, 0)),
                   pl.BlockSpec((bm, LANES), lambda i: (i, 0))],
        out_shape=[jax.ShapeDtypeStruct((m, d), BF16),
                   jax.ShapeDtypeStruct((m, LANES), F32)],
        compiler_params=_params("parallel"),
        name="prep",
    )(x, g.reshape(1, d))


def _final_norm_kernel(x_ref, g_ref, o_ref):
    x = x_ref[...]
    y = x * lax.rsqrt(jnp.mean(x * x, axis=-1, keepdims=True) + EPS)
    o_ref[...] = y * g_ref[...]


def _final_norm(x, g):
    m, d = x.shape
    bm = _tile(m, ROW_TILE)
    return pl.pallas_call(
        _final_norm_kernel,
        grid=(m // bm,),
        in_specs=[pl.BlockSpec((bm, d), lambda i: (i, 0)),
                  pl.BlockSpec((1, d), lambda i: (0, 0))],
        out_specs=pl.BlockSpec((bm, d), lambda i: (i, 0)),
        out_shape=jax.ShapeDtypeStruct((m, d), F32),
        compiler_params=_params("parallel"),
        name="final_norm",
    )(x, g.reshape(1, d))


def _mm_kernel(*refs, nk, has_ssq, has_res, has_gain, n_casts, cast_w, act, n_groups,
               post_scale, inv_k):
    refs = list(refs)
    a_ref, w_ref = refs[0], refs[1]
    pos = 2
    ssq_ref = res_ref = gain_ref = None
    if has_ssq:
        ssq_ref = refs[pos]
        pos += 1
    if has_res:
        res_ref = refs[pos]
        pos += 1
    if has_gain:
        gain_ref = refs[pos]
        pos += 1
    cast_in = refs[pos:pos + n_casts]
    pos += n_casts
    o_ref = refs[pos]
    pos += 1
    xb_ref = ssq_out_ref = None
    if has_gain:
        xb_ref, ssq_out_ref = refs[pos], refs[pos + 1]
        pos += 2
    cast_out = refs[pos:pos + n_casts]
    pos += n_casts
    acc_ref = refs[pos] if nk > 1 else None

    def cast_slabs():
        for ci, co in zip(cast_in, cast_out):
            if len(co.shape) == 2:
                co[...] = ci[...].astype(BF16)
            else:
                cb = co.shape[2]
                for c in range(co.shape[0]):
                    co[c] = ci[:, c * cb:(c + 1) * cb].astype(BF16)

    if nk == 1:
        cast_slabs()
    elif n_casts:
        pl.when(pl.program_id(2) == 0)(cast_slabs)

    def emit(new):
        xb_ref[...] = (new * gain_ref[...]).astype(BF16)
        ssq_out_ref[...] = _fold_lanes(new * new)

    def dot():
        w = w_ref[...]
        if cast_w:
            w = w.astype(BF16)
        return jnp.dot(a_ref[...], w, preferred_element_type=F32)

    if nk > 1:
        assert nk % 2 == 0 and has_res and not has_ssq and act is None and post_scale is None
        k = pl.program_id(2)

        @pl.when(k == 0)
        def _():
            acc_ref[...] = res_ref[...] + dot()

        @pl.when(jnp.logical_and(k > 0, k % 2 == 0))
        def _():
            acc_ref[...] = o_ref[...] + dot()

        @pl.when(jnp.logical_and(k % 2 == 1, k < nk - 1))
        def _():
            o_ref[...] = acc_ref[...] + dot()

        @pl.when(k == nk - 1)
        def _():
            new = acc_ref[...] + dot()
            o_ref[...] = new
            if has_gain:
                emit(new)
        return

    acc = dot()
    if has_ssq:
        acc = acc * lax.rsqrt(jnp.sum(ssq_ref[...], axis=-1, keepdims=True) * inv_k + EPS)
    if post_scale is not None:
        acc = acc * post_scale
    if act == "relu2":
        acc = jnp.square(jnp.maximum(acc, 0.0))
    elif act == "group_softmax":
        gw = acc.shape[1] // n_groups
        parts = []
        for gi in range(n_groups):
            s = acc[:, gi * gw:(gi + 1) * gw]
            e = jnp.exp(s - jnp.max(s, axis=-1, keepdims=True))
            parts.append(e / jnp.sum(e, axis=-1, keepdims=True))
        acc = jnp.concatenate(parts, axis=1)
    if has_res:
        acc = acc + res_ref[...]
    o_ref[...] = acc.astype(o_ref.dtype)
    if has_gain:
        emit(acc)


def _mm(a, w, *, layer=None, ssq=None, post_scale=None, res=None, act=None, n_groups=1,
        gain=None, casts=(), out_dtype=BF16, bm=None, bn=None, bk=None, name="mm"):
    m, kdim = a.shape
    nl, _, n = w.shape
    if layer == "blocked":
        bn, n = n, nl * n
    rows_per_w = m if layer is not None else m // nl
    bm = _tile(rows_per_w, bm or MM_BM)
    bn = _tile(n, bn or MM_BN)
    bk = _tile(kdim, bk or MM_BK)
    nk = kdim // bk
    blocks_per_w = rows_per_w // bm
    if layer == "blocked":
        w_map = lambda i, j, k: (j, k, 0)
    elif layer is not None:
        w_map = lambda i, j, k: (layer, k, j)
    else:
        w_map = lambda i, j, k: (i // blocks_per_w, k, j)

    in_specs = [pl.BlockSpec((bm, bk), lambda i, j, k: (i, k)),
                pl.BlockSpec((None, bk, bn), w_map)]
    args = [a, w]
    if ssq is not None:
        in_specs.append(pl.BlockSpec((bm, ssq.shape[1]), lambda i, j, k: (i, 0)))
        args.append(ssq)
    if res is not None:
        in_specs.append(pl.BlockSpec((bm, bn), lambda i, j, k: (i, j)))
        args.append(res)
    out_specs = [pl.BlockSpec((bm, bn), lambda i, j, k: (i, j))]
    out_shape = [jax.ShapeDtypeStruct((m, n), out_dtype)]
    if gain is not None:
        in_specs.append(pl.BlockSpec((1, bn), lambda i, j, k: (0, j)))
        args.append(gain.reshape(1, n))
        out_specs += [pl.BlockSpec((bm, bn), lambda i, j, k: (i, j)),
                      pl.BlockSpec((bm, LANES), lambda i, j, k: (i, j))]
        out_shape += [jax.ShapeDtypeStruct((m, n), BF16),
                      jax.ShapeDtypeStruct((m, LANES * (n // bn)), F32)]
    n_main = len(out_shape)
    gi, gj = m // bm, n // bn
    for src, src_layer, *col_block in casts:
        _, kw, nw = src.shape
        cj = max(c for c in range(1, gj + 1)
                 if kw % (gi * c) == 0 and (kw // (gi * c)) % BF16_SUBLANES == 0)
        rows = kw // (gi * cj)
        slab = lambda i, j, k, cj=cj: i * cj + jnp.minimum(j, cj - 1)
        in_specs.append(pl.BlockSpec(
            (None, rows, nw), lambda i, j, k, sl=slab, l=src_layer: (l, sl(i, j, k), 0)))
        args.append(src)
        nblk = nw // col_block[0] if col_block else 1
        first = nblk if col_block else None
        out_specs.append(pl.BlockSpec(
            (first, rows, nw // nblk), lambda i, j, k, sl=slab: (0, sl(i, j, k), 0)))
        out_shape.append(jax.ShapeDtypeStruct((nblk, kw, nw // nblk), BF16))
    kern = functools.partial(
        _mm_kernel, nk=nk, has_ssq=ssq is not None, has_res=res is not None,
        has_gain=gain is not None, n_casts=len(casts), cast_w=w.dtype != BF16, act=act,
        n_groups=n_groups, post_scale=post_scale, inv_k=1.0 / kdim)
    outs = pl.pallas_call(
        kern,
        grid=(gi, gj, nk),
        in_specs=in_specs,
        out_specs=out_specs,
        out_shape=out_shape,
        scratch_shapes=[pltpu.VMEM((bm, bn), F32)] if nk > 1 else [],
        compiler_params=_params("parallel", "arbitrary", "arbitrary"),
        name=name,
    )(*args)
    main =tuple(outs[:n_main]) if gain is not None else outs[0]
    return main, list(outs[n_main:])


def _mix_rows(z_ref, halo_ref, pw_ref, ps_ref, lg_ref, lb_ref, sw_ref, sb_ref, y_ref, v_ref,
              pos0, wpool, wsgu):
    t = z_ref.shape[0]
    n_groups = len(POOL_WINDOWS)
    gc = wpool // n_groups

    halo = halo_ref[...]
    halo = jnp.where(pos0 == 0, jnp.zeros_like(halo), halo)
    row = lax.broadcasted_iota(jnp.int32, (t, t + POOL_HALO), 0)
    col = lax.broadcasted_iota(jnp.int32, (t, t + POOL_HALO), 1)
    lag = row + POOL_HALO - col
    pos = pos0 + lax.broadcasted_iota(jnp.int32, (t, 1), 0)
    for gi, win in enumerate(POOL_WINDOWS):
        cs = slice(gi * gc, (gi + 1) * gc)
        zg = z_ref[:, cs]
        band = jnp.logical_and(lag >= 0, lag < win).astype(BF16)
        wsum = jnp.dot(band, jnp.concatenate([halo[:, cs], zg], axis=0),
                       preferred_element_type=F32)
        cnt = jnp.minimum(pos + 1, win).astype(F32)
        d = wsum / cnt - zg.astype(F32)
        yg = jnp.dot(d.astype(BF16), pw_ref[gi], preferred_element_type=F32)
        y_ref[:, cs] = (yg * ps_ref[:, cs]).astype(BF16)

    gv = jax.nn.gelu(z_ref[:, wpool + wsgu:].astype(F32))
    mu = jnp.mean(gv, axis=-1, keepdims=True)
    cen = gv - mu
    var = jnp.mean(cen * cen, axis=-1, keepdims=True)
    v_ref[...] = (cen * lax.rsqrt(var + EPS) * lg_ref[...] + lb_ref[...]).astype(BF16)

    tri = (lax.broadcasted_iota(jnp.int32, (CHUNK, CHUNK), 0)
           >= lax.broadcasted_iota(jnp.int32, (CHUNK, CHUNK), 1))
    for h in range(wsgu // SGU_HEAD):
        hs = slice(h * SGU_HEAD, (h + 1) * SGU_HEAD)
        us = slice(wpool + h * SGU_HEAD, wpool + (h + 1) * SGU_HEAD)
        wm = jnp.where(tri, sw_ref[h], jnp.zeros((CHUNK, CHUNK), BF16))
        for c in range(t // CHUNK):
            rs = slice(c * CHUNK, (c + 1) * CHUNK)
            mixed = jnp.dot(wm, v_ref[rs, hs], preferred_element_type=F32) + sb_ref[:, hs]
            u = jax.nn.gelu(z_ref[rs, us].astype(F32))
            y_ref[rs, us] = (u * mixed).astype(BF16)


def _mixout_kernel(*refs, seq, wpool, wsgu, n_tiles, has_cast):
    (z_ref, halo_ref, pw_ref, ps_ref, lg_ref, lb_ref, sw_ref, sb_ref,
     w_ref, res_ref, gain_ref, y0_hbm) = refs[:12]
    pos = 12
    cast_in = cast_out = None
    if has_cast:
        cast_in = refs[pos]
        pos += 1
    o_ref, xb_ref, ssq_out_ref = refs[pos:pos + 3]
    pos += 3
    if has_cast:
        cast_out = refs[pos]
        pos += 1
    y_even, y_odd, v_ref, y0_sem = refs[pos:pos + 4]
    i, j = pl.program_id(0), pl.program_id(1)
    bm = y_even.shape[0]

    if has_cast:
        cb = cast_out.shape[2]
        for c in range(cast_out.shape[0]):
            cast_out[c] = cast_in[:, c * cb:(c + 1) * cb].astype(BF16)

    @pl.when(jnp.logical_and(i == 0, j == 0))
    def _():
        y0_copy = pltpu.make_async_copy(y0_hbm, y_even, y0_sem)
        y0_copy.start()
        y0_copy.wait()

    t = z_ref.shape[0]
    row0 = jnp.minimum(i + 1, n_tiles - 1) * bm + j * t

    def step(y_read, y_write):
        chunk = y_write.at[pl.ds(pl.multiple_of(j * t, t), t), :]
        _mix_rows(z_ref, halo_ref, pw_ref, ps_ref, lg_ref, lb_ref, sw_ref, sb_ref, chunk,
                  v_ref, row0 % seq, wpool, wsgu)
        new = jnp.dot(y_read[...], w_ref[...], preferred_element_type=F32) + res_ref[...]
        o_ref[...] = new
        xb_ref[...] = (new * gain_ref[...]).astype(BF16)
        ssq_out_ref[...] = _fold_lanes(new * new)

    @pl.when(i % 2 == 0)
    def _():
        step(y_even, y_odd)

    @pl.when(i % 2 == 1)
    def _():
        step(y_odd, y_even)


def _mixout_bn(seq, n):
    gj = _tile(seq, MM_BM) // _tile(seq, MIXOUT_ROWS)
    assert n % (gj * LANES) == 0
    return n // gj


def _mixout(z, pool_w, pool_scale, ln_g, ln_b, sgu_w, sgu_b, w, res, gain, *, layer, seq,
            cast=None):
    m, d_in = z.shape
    _, n_groups, gc, _ = pool_w.shape
    wpool = n_groups * gc
    wsgu = (d_in - wpool) // 2
    n_heads = wsgu // SGU_HEAD
    assert n_groups == len(POOL_WINDOWS) and sgu_w.shape[1:] == (n_heads, CHUNK, CHUNK)
    gj, kdim, bn = w.shape
    n = gj * bn
    bm = _tile(seq, MM_BM)
    t = bm // gj
    n_tiles = m // bm
    assert kdim == wpool + wsgu and bn == _mixout_bn(seq, n)
    assert t % CHUNK == 0 and t % POOL_HALO == 0
    halo_blocks = t // POOL_HALO
    sb = jnp.repeat(jnp.transpose(sgu_b), SGU_HEAD, axis=1)

    y0 = _mix(z, pool_w, pool_scale, ln_g, ln_b, sgu_w, sgu_b, layer=layer, seq=seq, rows=bm)

    chunk_idx = lambda i, j: jnp.minimum(i + 1, n_tiles - 1) * gj + j
    const2 = lambda i, j: (0, 0)
    once = dict(pipeline_mode=pl.Buffered(1))
    in_specs = [pl.BlockSpec((t, d_in), lambda i, j: (chunk_idx(i, j), 0)),
                pl.BlockSpec((POOL_HALO, wpool),
                             lambda i, j: (jnp.maximum(chunk_idx(i, j) * halo_blocks - 1, 0), 0)),
                pl.BlockSpec((None, n_groups, gc, gc), lambda i, j: (layer, 0, 0, 0), **once),
                pl.BlockSpec((1, wpool), const2, **once),
                pl.BlockSpec((1, wsgu), const2, **once),
                pl.BlockSpec((1, wsgu), const2, **once),
                pl.BlockSpec((None, n_heads, CHUNK, CHUNK), lambda i, j: (layer, 0, 0, 0),
                             **once),
                pl.BlockSpec((CHUNK, wsgu), const2, **once),
                pl.BlockSpec((None, kdim, bn), lambda i, j: (j, 0, 0)),
                pl.BlockSpec((bm, bn), lambda i, j: (i, j)),
                pl.BlockSpec((1, bn), lambda i, j: (0, j)),
                pl.BlockSpec(memory_space=pl.ANY)]
    args = [z, z, pool_w, pool_scale.reshape(1, wpool), ln_g.reshape(1, wsgu),
            ln_b.reshape(1, wsgu), sgu_w, sb, w, res, gain.reshape(1, n), y0]
    out_specs = [pl.BlockSpec((bm, bn), lambda i, j: (i, j)),
                 pl.BlockSpec((bm, bn), lambda i, j: (i, j)),
                 pl.BlockSpec((bm, LANES), lambda i, j: (i, j))]
    out_shape = [jax.ShapeDtypeStruct((m, n), F32),
                 jax.ShapeDtypeStruct((m, n), BF16),
                 jax.ShapeDtypeStruct((m, LANES * gj), F32)]
    if cast is not None:
        src, src_layer, cb = cast
        _, kw, nw = src.shape
        steps = n_tiles * gj
        cs = max(c for c in range(1, steps + 1)
                 if kw % c == 0 and (kw // c) % BF16_SUBLANES == 0)
        slab = lambda i, j: jnp.minimum(i * gj + j, cs - 1)
        in_specs.append(pl.BlockSpec((None, kw // cs, nw),
                                     lambda i, j: (src_layer, slab(i, j), 0)))
        args.append(src)
        out_specs.append(pl.BlockSpec((nw // cb, kw // cs, cb),
                                      lambda i, j: (0, slab(i, j), 0)))
        out_shape.append(jax.ShapeDtypeStruct((nw // cb, kw, cb), BF16))
    kern = functools.partial(_mixout_kernel, seq=seq, wpool=wpool, wsgu=wsgu, n_tiles=n_tiles,
                             has_cast=cast is not None)
    outs = pl.pallas_call(
        kern,
        grid=(n_tiles, gj),
        in_specs=in_specs,
        out_specs=out_specs,
        out_shape=out_shape,
        scratch_shapes=[pltpu.VMEM((bm, kdim), BF16), pltpu.VMEM((bm, kdim), BF16),
                        pltpu.VMEM((t, wsgu), BF16), pltpu.SemaphoreType.DMA(())],
        compiler_params=_params("arbitrary", "arbitrary"),
        name="mixout",
    )(*args)
    return outs[0], outs[1], outs[2], (outs[3] if cast is not None else None)


def _mix_kernel(z_ref, halo_ref, pw_ref, ps_ref, lg_ref, lb_ref, sw_ref, sb_ref,
                y_ref, v_ref, *, seq, wpool, wsgu):
    pos0 = (pl.program_id(0) * z_ref.shape[0]) % seq
    _mix_rows(z_ref, halo_ref, pw_ref, ps_ref, lg_ref, lb_ref, sw_ref, sb_ref, y_ref, v_ref,
              pos0, wpool, wsgu)


def _mix(z, pool_w, pool_scale, ln_g, ln_b, sgu_w, sgu_b, *, layer, seq, rows):
    m, d_in = rows, z.shape[1]
    _, n_groups, gc, _ = pool_w.shape
    wpool = n_groups * gc
    wsgu = (d_in - wpool) // 2
    n_heads = wsgu // SGU_HEAD
    assert n_groups == len(POOL_WINDOWS) and sgu_w.shape[1:] == (n_heads, CHUNK, CHUNK)
    t = _tile(seq, ROW_TILE)
    assert t % CHUNK == 0 and t % POOL_HALO == 0
    halo_blocks = t // POOL_HALO
    sb = jnp.repeat(jnp.transpose(sgu_b), SGU_HEAD, axis=1)
    kern = functools.partial(_mix_kernel, seq=seq, wpool=wpool, wsgu=wsgu)
    const2 = lambda i: (0, 0)
    return pl.pallas_call(
        kern,
        grid=(m // t,),
        in_specs=[pl.BlockSpec((t, d_in), lambda i: (i, 0)),
                  pl.BlockSpec((POOL_HALO, wpool),
                               lambda i: (jnp.maximum(i * halo_blocks - 1, 0), 0)),
                  pl.BlockSpec((None, n_groups, gc, gc), lambda i: (layer, 0, 0, 0)),
                  pl.BlockSpec((1, wpool), const2),
                  pl.BlockSpec((1, wsgu), const2),
                  pl.BlockSpec((1, wsgu), const2),
                  pl.BlockSpec((None, n_heads, CHUNK, CHUNK), lambda i: (layer, 0, 0, 0)),
                  pl.BlockSpec((CHUNK, wsgu), const2)],
        out_specs=pl.BlockSpec((t, wpool + wsgu), lambda i: (i, 0)),
        out_shape=jax.ShapeDtypeStruct((m, wpool + wsgu), BF16),
        scratch_shapes=[pltpu.VMEM((t, wsgu), BF16)],
        compiler_params=_params("parallel"),
        name="mix",
    )(z, z, pool_w, pool_scale.reshape(1, wpool), ln_g.reshape(1, wsgu),
      ln_b.reshape(1, wsgu), sgu_w, sb)


def _fold_qk_kernel(wq_ref, k_ref, a_ref):
    n_mem = a_ref.shape[2]
    s = lax.dot_general(wq_ref[...].astype(BF16), k_ref[...], (((1,), (1,)), ((), ())),
                        preferred_element_type=F32)
    for b in range(a_ref.shape[0]):
        a_ref[b] = s[:, b * n_mem:(b + 1) * n_mem].astype(a_ref.dtype)


def _fold_vo_kernel(v_ref, wo_ref, b_ref):
    n_mem = b_ref.shape[1]
    s = jnp.dot(v_ref[...], wo_ref[...].astype(BF16), preferred_element_type=F32)
    for b in range(b_ref.shape[0]):
        b_ref[b] = s[b * n_mem:(b + 1) * n_mem].astype(b_ref.dtype)


def _fold_qk(wq, k, *, layer, n_batch, n_mem):
    d = wq.shape[1]
    dh = d // N_XHEADS
    rb = _tile(d, FOLD_TILE)
    return pl.pallas_call(
        _fold_qk_kernel,
        grid=(N_XHEADS, d // rb),
        in_specs=[pl.BlockSpec((None, rb, dh), lambda h, r: (layer, r, h)),
                  pl.BlockSpec((n_batch * n_mem, dh), lambda h, r: (0, h))],
        out_specs=pl.BlockSpec((n_batch, rb, n_mem), lambda h, r: (0, r, h)),
        out_shape=jax.ShapeDtypeStruct((n_batch, d, N_XHEADS * n_mem), BF16),
        compiler_params=_params("parallel", "parallel"),
        name="fold_qk",
    )(wq, k)


def _fold_vo(v, wo, *, layer, n_batch, n_mem):
    d = wo.shape[2]
    dh = d // N_XHEADS
    cb = _tile(d, FOLD_TILE)
    return pl.pallas_call(
        _fold_vo_kernel,
        grid=(N_XHEADS, d // cb),
        in_specs=[pl.BlockSpec((n_batch * n_mem, dh), lambda h, c: (0, h)),
                  pl.BlockSpec((None, dh, cb), lambda h, c: (layer, h, c))],
        out_specs=pl.BlockSpec((n_batch, n_mem, cb), lambda h, c: (0, h, c)),
        out_shape=jax.ShapeDtypeStruct((n_batch, N_XHEADS * n_mem, d), BF16),
        compiler_params=_params("parallel", "parallel"),
        name="fold_vo",
    )(v, wo)


def kernel(x, mem, ln_mix, w_in, pool_w, pool_scale, sgu_ln_g, sgu_ln_b, sgu_w, sgu_b,
           w_out, ln_x, ln_mem, w_q, w_k, w_v, w_o, ln_ffn, w_up, w_down, ln_final):
    n_batch, seq, d = x.shape
    n_mem = mem.shape[1]
    depth = w_in.shape[0]
    dh = d // N_XHEADS
    x = x.reshape(n_batch * seq, d)
    memf = mem.reshape(n_batch * n_mem, d)

    wb_in = w_in[:1].astype(BF16)
    pool_w, sgu_w = pool_w.astype(BF16), sgu_w.astype(BF16)

    blocked = lambda w, l: (w, l, min(MM_BN, w.shape[2]))

    xb, ssq = _prep(x, ln_mix[0])
    for l in range(depth):
        z, (wb_out,) = _mm(xb, wb_in, layer=0 if l == 0 else "blocked", ssq=ssq,
                           casts=[(w_out, l, _mixout_bn(seq, d))], name="mm_in")
        x, xb, ssq, cast = _mixout(z, pool_w, pool_scale[l], sgu_ln_g[l], sgu_ln_b[l], sgu_w,
                                   sgu_b[l], wb_out, x, ln_x[l], layer=l, seq=seq,
                                   cast=blocked(w_up, 0) if l == 0 else None)
        if l == 0:
            wb_up = cast

        mb, mssq = _prep(memf, ln_mem[l])
        k, _ = _mm(mb, w_k, layer=l, ssq=mssq, bn=MM_BN_RES, name="mm_k")
        v, _ = _mm(mb, w_v, layer=l, ssq=mssq, bn=MM_BN_RES, name="mm_v")
        a_fold = _fold_qk(w_q, k, layer=l, n_batch=n_batch, n_mem=n_mem)
        b_fold = _fold_vo(v, w_o, layer=l, n_batch=n_batch, n_mem=n_mem)
        p, _ = _mm(xb, a_fold, ssq=ssq, post_scale=dh ** -0.5, act="group_softmax",
                   n_groups=N_XHEADS, bn=N_XHEADS * n_mem, name="mm_scores")
        (x, xb, ssq), _ = _mm(p, b_fold, res=x, gain=ln_ffn[l], out_dtype=F32,
                              bm=ATTN_OUT_BM, bn=d, name="mm_attn_out")

        last = l + 1 == depth
        nxt = [] if last else [blocked(w_up, l + 1), blocked(w_in, l + 1)]
        hid, cast = _mm(xb, wb_up, layer="blocked", ssq=ssq, act="relu2",
                        casts=[blocked(w_down, l)] + nxt, name="mm_up")
        wb_down = cast[0]
        if last:
            x, _ = _mm(hid, wb_down, layer="blocked", res=x, out_dtype=F32,
                       bk=MM_BK_DOWN_LAST, name="mm_down")
        else:
            wb_up, wb_in = cast[1:]
            (x, xb, ssq), _ = _mm(hid, wb_down, layer="blocked", res=x, gain=ln_mix[l + 1],
                                  out_dtype=F32, bk=MM_BK_DOWN, name="mm_down")

    return _final_norm(x, ln_final).reshape(n_batch, seq, d)
```

```python
import functools

import jax
import jax.numpy as jnp
from jax import lax
from jax.experimental import pallas as pl
from jax.experimental.pallas import tpu as pltpu

EPS = 1e-6
POOL_WINDOWS = (2, 4, 8, 16)
POOL_HALO = 16
CHUNK = 128
SGU_HEAD = 128
N_XHEADS = 4
LANES = 128
BF16_SUBLANES = 16

V7X_VMEM_LIMIT_BYTES = 63 * 1024 * 1024

MM_BM = 1024
MM_BN = 1024
MM_BN_RES = 512
MM_BK = 4096
MM_BK_DOWN = 4096
MM_BK_DOWN_LAST = 4096
ATTN_OUT_BM = 512
MIXOUT_ROWS = 128
ROW_TILE = 256
FOLD_TILE = 1024

F32 = jnp.float32
BF16 = jnp.bfloat16


def _tile(dim, pref):
    t = min(dim, pref)
    assert dim % t == 0, (dim, pref)
    return t


def _params(*sem):
    return pltpu.CompilerParams(dimension_semantics=sem,
                                vmem_limit_bytes=V7X_VMEM_LIMIT_BYTES)


def _fold_lanes(v):
    acc = v[:, :LANES]
    for c in range(1, v.shape[1] // LANES):
        acc = acc + v[:, c * LANES:(c + 1) * LANES]
    return acc


def _prep_kernel(x_ref, g_ref, xb_ref, ssq_ref):
    x = x_ref[...]
    ssq_ref[...] = _fold_lanes(x * x)
    xb_ref[...] = (x * g_ref[...]).astype(BF16)


def _prep(x, g):
    m, d = x.shape
    bm = _tile(m, ROW_TILE)
    return pl.pallas_call(
        _prep_kernel,
        grid=(m // bm,),
        in_specs=[pl.BlockSpec((bm, d), lambda i: (i, 0)),
                  pl.BlockSpec((1, d), lambda i: (0, 0))],
        out_specs=[pl.BlockSpec((bm, d), lambda i: (i, 0)),
                   pl.BlockSpec((bm, LANES), lambda i: (i, 0))],
        out_shape=[jax.ShapeDtypeStruct((m, d), BF16),
                   jax.ShapeDtypeStruct((m, LANES), F32)],
        compiler_params=_params("parallel"),
        name="prep",
    )(x, g.reshape(1, d))


def _final_norm_kernel(x_ref, g_ref, o_ref):
    x = x_ref[...]
    y = x * lax.rsqrt(jnp.mean(x * x, axis=-1, keepdims=True) + EPS)
    o_ref[...] = y * g_ref[...]


def _final_norm(x, g):
    m, d = x.shape
    bm = _tile(m, ROW_TILE)
    return pl.pallas_call(
        _final_norm_kernel,
        grid=(m // bm,),
        in_specs=[pl.BlockSpec((bm, d), lambda i: (i, 0)),
                  pl.BlockSpec((1, d), lambda i: (0, 0))],
        out_specs=pl.BlockSpec((bm, d), lambda i: (i, 0)),
        out_shape=jax.ShapeDtypeStruct((m, d), F32),
        compiler_params=_params("parallel"),
        name="final_norm",
    )(x, g.reshape(1, d))


def _mm_kernel(*refs, nk, has_ssq, has_res, has_gain, n_casts, cast_w, act, n_groups,
               post_scale, inv_k):
    refs = list(refs)
    a_ref, w_ref = refs[0], refs[1]
    pos = 2
    ssq_ref = res_ref = gain_ref = None
    if has_ssq:
        ssq_ref = refs[pos]
        pos += 1
    if has_res:
        res_ref = refs[pos]
        pos += 1
    if has_gain:
        gain_ref = refs[pos]
        pos += 1
    cast_in = refs[pos:pos + n_casts]
    pos += n_casts
    o_ref = refs[pos]
    pos += 1
    xb_ref = ssq_out_ref = None
    if has_gain:
        xb_ref, ssq_out_ref = refs[pos], refs[pos + 1]
        pos += 2
    cast_out = refs[pos:pos + n_casts]
    pos += n_casts
    acc_ref = refs[pos] if nk > 1 else None

    def cast_slabs():
        for ci, co in zip(cast_in, cast_out):
            if len(co.shape) == 2:
                co[...] = ci[...].astype(BF16)
            else:
                cb = co.shape[2]
                for c in range(co.shape[0]):
                    co[c] = ci[:, c * cb:(c + 1) * cb].astype(BF16)

    if nk == 1:
        cast_slabs()
    elif n_casts:
        pl.when(pl.program_id(2) == 0)(cast_slabs)

    def emit(new):
        xb_ref[...] = (new * gain_ref[...]).astype(BF16)
        ssq_out_ref[...] = _fold_lanes(new * new)

    def dot():
        w = w_ref[...]
        if cast_w:
            w = w.astype(BF16)
        return jnp.dot(a_ref[...], w, preferred_element_type=F32)

    if nk > 1:
        assert nk % 2 == 0 and has_res and not has_ssq and act is None and post_scale is None
        k = pl.program_id(2)

        @pl.when(k == 0)
        def _():
            acc_ref[...] = res_ref[...] + dot()

        @pl.when(jnp.logical_and(k > 0, k % 2 == 0))
        def _():
            acc_ref[...] = o_ref[...] + dot()

        @pl.when(jnp.logical_and(k % 2 == 1, k < nk - 1))
        def _():
            o_ref[...] = acc_ref[...] + dot()

        @pl.when(k == nk - 1)
        def _():
            new = acc_ref[...] + dot()
            o_ref[...] = new
            if has_gain:
                emit(new)
        return

    acc = dot()
    if has_ssq:
        acc = acc * lax.rsqrt(jnp.sum(ssq_ref[...], axis=-1, keepdims=True) * inv_k + EPS)
    if post_scale is not None:
        acc = acc * post_scale
    if act == "relu2":
        acc = jnp.square(jnp.maximum(acc, 0.0))
    elif act == "group_softmax":
        gw = acc.shape[1] // n_groups
        parts = []
        for gi in range(n_groups):
            s = acc[:, gi * gw:(gi + 1) * gw]
            e = jnp.exp(s - jnp.max(s, axis=-1, keepdims=True))
            parts.append(e / jnp.sum(e, axis=-1, keepdims=True))
        acc = jnp.concatenate(parts, axis=1)
    if has_res:
        acc = acc + res_ref[...]
    o_ref[...] = acc.astype(o_ref.dtype)
    if has_gain:
        emit(acc)


def _mm(a, w, *, layer=None, ssq=None, post_scale=None, res=None, act=None, n_groups=1,
        gain=None, casts=(), out_dtype=BF16, bm=None, bn=None, bk=None, name="mm"):
    m, kdim = a.shape
    nl, _, n = w.shape
    if layer == "blocked":
        bn, n = n, nl * n
    rows_per_w = m if layer is not None else m // nl
    bm = _tile(rows_per_w, bm or MM_BM)
    bn = _tile(n, bn or MM_BN)
    bk = _tile(kdim, bk or MM_BK)
    nk = kdim // bk
    blocks_per_w = rows_per_w // bm
    if layer == "blocked":
        w_map = lambda i, j, k: (j, k, 0)
    elif layer is not None:
        w_map = lambda i, j, k: (layer, k, j)
    else:
        w_map = lambda i, j, k: (i // blocks_per_w, k, j)

    in_specs = [pl.BlockSpec((bm, bk), lambda i, j, k: (i, k)),
                pl.BlockSpec((None, bk, bn), w_map)]
    args = [a, w]
    if ssq is not None:
        in_specs.append(pl.BlockSpec((bm, ssq.shape[1]), lambda i, j, k: (i, 0)))
        args.append(ssq)
    if res is not None:
        in_specs.append(pl.BlockSpec((bm, bn), lambda i, j, k: (i, j)))
        args.append(res)
    out_specs = [pl.BlockSpec((bm, bn), lambda i, j, k: (i, j))]
    out_shape = [jax.ShapeDtypeStruct((m, n), out_dtype)]
    if gain is not None:
        in_specs.append(pl.BlockSpec((1, bn), lambda i, j, k: (0, j)))
        args.append(gain.reshape(1, n))
        mode = dict(pipeline_mode=pl.Buffered(1)) if nk > 1 else {}
        out_specs += [pl.BlockSpec((bm, bn), lambda i, j, k: (i, j), **mode),
                      pl.BlockSpec((bm, LANES), lambda i, j, k: (i, j), **mode)]
        out_shape += [jax.ShapeDtypeStruct((m, n), BF16),
                      jax.ShapeDtypeStruct((m, LANES * (n // bn)), F32)]
    n_main = len(out_shape)
    gi, gj = m // bm, n // bn
    for src, src_layer, *col_block in casts:
        _, kw, nw = src.shape
        cj = max(c for c in range(1, gj + 1)
                 if kw % (gi * c) == 0 and (kw // (gi * c)) % BF16_SUBLANES == 0)
        rows = kw // (gi * cj)
        slab = lambda i, j, k, cj=cj: i * cj + jnp.minimum(j, cj - 1)
        in_specs.append(pl.BlockSpec(
            (None, rows, nw), lambda i, j, k, sl=slab, l=src_layer: (l, sl(i, j, k), 0)))
        args.append(src)
        nblk = nw // col_block[0] if col_block else 1
        first = nblk if col_block else None
        out_specs.append(pl.BlockSpec(
            (first, rows, nw // nblk), lambda i, j, k, sl=slab: (0, sl(i, j, k), 0)))
        out_shape.append(jax.ShapeDtypeStruct((nblk, kw, nw // nblk), BF16))
    kern = functools.partial(
        _mm_kernel, nk=nk, has_ssq=ssq is not None, has_res=res is not None,
        has_gain=gain is not None, n_casts=len(casts), cast_w=w.dtype != BF16, act=act,
        n_groups=n_groups, post_scale=post_scale, inv_k=1.0 / kdim)
    outs = pl.pallas_call(
        kern,
        grid=(gi, gj, nk),
        in_specs=in_specs,
        out_specs=out_specs,
        out_shape=out_shape,
        scratch_shapes=[pltpu.VMEM((bm, bn), F32)] if nk > 1 else [],
        compiler_params=_params("parallel", "arbitrary", "arbitrary"),
        name=name,
    )(*args)
    main =tuple(outs[:n_main]) if gain is not None else outs[0]
    return main, list(outs[n_main:])


def _mix_rows(z_ref, halo_ref, pw_ref, ps_ref, lg_ref, lb_ref, sw_ref, sb_ref, y_ref, v_ref,
              pos0, wpool, wsgu):
    t = z_ref.shape[0]
    n_groups = len(POOL_WINDOWS)
    gc = wpool // n_groups

    halo = halo_ref[...]
    halo = jnp.where(pos0 == 0, jnp.zeros_like(halo), halo)
    row = lax.broadcasted_iota(jnp.int32, (t, t + POOL_HALO), 0)
    col = lax.broadcasted_iota(jnp.int32, (t, t + POOL_HALO), 1)
    lag = row + POOL_HALO - col
    pos = pos0 + lax.broadcasted_iota(jnp.int32, (t, 1), 0)
    for gi, win in enumerate(POOL_WINDOWS):
        cs = slice(gi * gc, (gi + 1) * gc)
        zg = z_ref[:, cs]
        band = jnp.logical_and(lag >= 0, lag < win).astype(BF16)
        wsum = jnp.dot(band, jnp.concatenate([halo[:, cs], zg], axis=0),
                       preferred_element_type=F32)
        cnt = jnp.minimum(pos + 1, win).astype(F32)
        d = wsum / cnt - zg.astype(F32)
        yg = jnp.dot(d.astype(BF16), pw_ref[gi], preferred_element_type=F32)
        y_ref[:, cs] = (yg * ps_ref[:, cs]).astype(BF16)

    gv = jax.nn.gelu(z_ref[:, wpool + wsgu:].astype(F32))
    mu = jnp.mean(gv, axis=-1, keepdims=True)
    cen = gv - mu
    var = jnp.mean(cen * cen, axis=-1, keepdims=True)
    v_ref[...] = (cen * lax.rsqrt(var + EPS) * lg_ref[...] + lb_ref[...]).astype(BF16)

    tri = (lax.broadcasted_iota(jnp.int32, (CHUNK, CHUNK), 0)
           >= lax.broadcasted_iota(jnp.int32, (CHUNK, CHUNK), 1))
    for h in range(wsgu // SGU_HEAD):
        hs = slice(h * SGU_HEAD, (h + 1) * SGU_HEAD)
        us = slice(wpool + h * SGU_HEAD, wpool + (h + 1) * SGU_HEAD)
        wm = jnp.where(tri, sw_ref[h], jnp.zeros((CHUNK, CHUNK), BF16))
        for c in range(t // CHUNK):
            rs = slice(c * CHUNK, (c + 1) * CHUNK)
            mixed = jnp.dot(wm, v_ref[rs, hs], preferred_element_type=F32) + sb_ref[:, hs]
            u = jax.nn.gelu(z_ref[rs, us].astype(F32))
            y_ref[rs, us] = (u * mixed).astype(BF16)


def _mixout_kernel(*refs, seq, wpool, wsgu, n_tiles, has_cast):
    (z_ref, halo_ref, pw_ref, ps_ref, lg_ref, lb_ref, sw_ref, sb_ref,
     w_ref, res_ref, gain_ref, y0_hbm) = refs[:12]
    pos = 12
    cast_in = cast_out = None
    if has_cast:
        cast_in = refs[pos]
        pos += 1
    o_ref, xb_ref, ssq_out_ref = refs[pos:pos + 3]
    pos += 3
    if has_cast:
        cast_out = refs[pos]
        pos += 1
    y_even, y_odd, v_ref, y0_sem = refs[pos:pos + 4]
    i, j = pl.program_id(0), pl.program_id(1)
    bm = y_even.shape[0]

    if has_cast:
        cb = cast_out.shape[2]
        for c in range(cast_out.shape[0]):
            cast_out[c] = cast_in[:, c * cb:(c + 1) * cb].astype(BF16)

    @pl.when(jnp.logical_and(i == 0, j == 0))
    def _():
        y0_copy = pltpu.make_async_copy(y0_hbm, y_even, y0_sem)
        y0_copy.start()
        y0_copy.wait()

    t = z_ref.shape[0]
    row0 = jnp.minimum(i + 1, n_tiles - 1) * bm + j * t

    def step(y_read, y_write):
        chunk = y_write.at[pl.ds(pl.multiple_of(j * t, t), t), :]
        _mix_rows(z_ref, halo_ref, pw_ref, ps_ref, lg_ref, lb_ref, sw_ref, sb_ref, chunk,
                  v_ref, row0 % seq, wpool, wsgu)
        new = jnp.dot(y_read[...], w_ref[...], preferred_element_type=F32) + res_ref[...]
        o_ref[...] = new
        xb_ref[...] = (new * gain_ref[...]).astype(BF16)
        ssq_out_ref[...] = _fold_lanes(new * new)

    @pl.when(i % 2 == 0)
    def _():
        step(y_even, y_odd)

    @pl.when(i % 2 == 1)
    def _():
        step(y_odd, y_even)


def _mixout_bn(seq, n):
    gj = _tile(seq, MM_BM) // _tile(seq, MIXOUT_ROWS)
    assert n % (gj * LANES) == 0
    return n // gj


def _mixout(z, pool_w, pool_scale, ln_g, ln_b, sgu_w, sgu_b, w, res, gain, *, layer, seq,
            cast=None):
    m, d_in = z.shape
    _, n_groups, gc, _ = pool_w.shape
    wpool = n_groups * gc
    wsgu = (d_in - wpool) // 2
    n_heads = wsgu // SGU_HEAD
    assert n_groups == len(POOL_WINDOWS) and sgu_w.shape[1:] == (n_heads, CHUNK, CHUNK)
    gj, kdim, bn = w.shape
    n = gj * bn
    bm = _tile(seq, MM_BM)
    t = bm // gj
    n_tiles = m // bm
    assert kdim == wpool + wsgu and bn == _mixout_bn(seq, n)
    assert t % CHUNK == 0 and t % POOL_HALO == 0
    halo_blocks = t // POOL_HALO
    sb = jnp.repeat(jnp.transpose(sgu_b), SGU_HEAD, axis=1)

    y0 = _mix(z, pool_w, pool_scale, ln_g, ln_b, sgu_w, sgu_b, layer=layer, seq=seq, rows=bm)

    chunk_idx = lambda i, j: jnp.minimum(i + 1, n_tiles - 1) * gj + j
    const2 = lambda i, j: (0, 0)
    once = dict(pipeline_mode=pl.Buffered(1))
    in_specs = [pl.BlockSpec((t, d_in), lambda i, j: (chunk_idx(i, j), 0)),
                pl.BlockSpec((POOL_HALO, wpool),
                             lambda i, j: (jnp.maximum(chunk_idx(i, j) * halo_blocks - 1, 0), 0)),
                pl.BlockSpec((None, n_groups, gc, gc), lambda i, j: (layer, 0, 0, 0), **once),
                pl.BlockSpec((1, wpool), const2, **once),
                pl.BlockSpec((1, wsgu), const2, **once),
                pl.BlockSpec((1, wsgu), const2, **once),
                pl.BlockSpec((None, n_heads, CHUNK, CHUNK), lambda i, j: (layer, 0, 0, 0),
                             **once),
                pl.BlockSpec((CHUNK, wsgu), const2, **once),
                pl.BlockSpec((None, kdim, bn), lambda i, j: (j, 0, 0)),
                pl.BlockSpec((bm, bn), lambda i, j: (i, j)),
                pl.BlockSpec((1, bn), lambda i, j: (0, j)),
                pl.BlockSpec(memory_space=pl.ANY)]
    args = [z, z, pool_w, pool_scale.reshape(1, wpool), ln_g.reshape(1, wsgu),
            ln_b.reshape(1, wsgu), sgu_w, sb, w, res, gain.reshape(1, n), y0]
    out_specs = [pl.BlockSpec((bm, bn), lambda i, j: (i, j)),
                 pl.BlockSpec((bm, bn), lambda i, j: (i, j)),
                 pl.BlockSpec((bm, LANES), lambda i, j: (i, j))]
    out_shape = [jax.ShapeDtypeStruct((m, n), F32),
                 jax.ShapeDtypeStruct((m, n), BF16),
                 jax.ShapeDtypeStruct((m, LANES * gj), F32)]
    if cast is not None:
        src, src_layer, cb = cast
        _, kw, nw = src.shape
        steps = n_tiles * gj
        cs = max(c for c in range(1, steps + 1)
                 if kw % c == 0 and (kw // c) % BF16_SUBLANES == 0)
        slab = lambda i, j: jnp.minimum(i * gj + j, cs - 1)
        in_specs.append(pl.BlockSpec((None, kw // cs, nw),
                                     lambda i, j: (src_layer, slab(i, j), 0)))
        args.append(src)
        out_specs.append(pl.BlockSpec((nw // cb, kw // cs, cb),
                                      lambda i, j: (0, slab(i, j), 0)))
        out_shape.append(jax.ShapeDtypeStruct((nw // cb, kw, cb), BF16))
    kern = functools.partial(_mixout_kernel, seq=seq, wpool=wpool, wsgu=wsgu, n_tiles=n_tiles,
                             has_cast=cast is not None)
    outs = pl.pallas_call(
        kern,
        grid=(n_tiles, gj),
        in_specs=in_specs,
        out_specs=out_specs,
        out_shape=out_shape,
        scratch_shapes=[pltpu.VMEM((bm, kdim), BF16), pltpu.VMEM((bm, kdim), BF16),
                        pltpu.VMEM((t, wsgu), BF16), pltpu.SemaphoreType.DMA(())],
        compiler_params=_params("arbitrary", "arbitrary"),
        name="mixout",
    )(*args)
    return outs[0], outs[1], outs[2], (outs[3] if cast is not None else None)


def _mix_kernel(z_ref, halo_ref, pw_ref, ps_ref, lg_ref, lb_ref, sw_ref, sb_ref,
                y_ref, v_ref, *, seq, wpool, wsgu):
    pos0 = (pl.program_id(0) * z_ref.shape[0]) % seq
    _mix_rows(z_ref, halo_ref, pw_ref, ps_ref, lg_ref, lb_ref, sw_ref, sb_ref, y_ref, v_ref,
              pos0, wpool, wsgu)


def _mix(z, pool_w, pool_scale, ln_g, ln_b, sgu_w, sgu_b, *, layer, seq, rows):
    m, d_in = rows, z.shape[1]
    _, n_groups, gc, _ = pool_w.shape
    wpool = n_groups * gc
    wsgu = (d_in - wpool) // 2
    n_heads = wsgu // SGU_HEAD
    assert n_groups == len(POOL_WINDOWS) and sgu_w.shape[1:] == (n_heads, CHUNK, CHUNK)
    t = _tile(seq, ROW_TILE)
    assert t % CHUNK == 0 and t % POOL_HALO == 0
    halo_blocks = t // POOL_HALO
    sb = jnp.repeat(jnp.transpose(sgu_b), SGU_HEAD, axis=1)
    kern = functools.partial(_mix_kernel, seq=seq, wpool=wpool, wsgu=wsgu)
    const2 = lambda i: (0, 0)
    return pl.pallas_call(
        kern,
        grid=(m // t,),
        in_specs=[pl.BlockSpec((t, d_in), lambda i: (i, 0)),
                  pl.BlockSpec((POOL_HALO, wpool),
                               lambda i: (jnp.maximum(i * halo_blocks - 1, 0), 0)),
                  pl.BlockSpec((None, n_groups, gc, gc), lambda i: (layer, 0, 0, 0)),
                  pl.BlockSpec((1, wpool), const2),
                  pl.BlockSpec((1, wsgu), const2),
                  pl.BlockSpec((1, wsgu), const2),
                  pl.BlockSpec((None, n_heads, CHUNK, CHUNK), lambda i: (layer, 0, 0, 0)),
                  pl.BlockSpec((CHUNK, wsgu), const2)],
        out_specs=pl.BlockSpec((t, wpool + wsgu), lambda i: (i, 0)),
        out_shape=jax.ShapeDtypeStruct((m, wpool + wsgu), BF16),
        scratch_shapes=[pltpu.VMEM((t, wsgu), BF16)],
        compiler_params=_params("parallel"),
        name="mix",
    )(z, z, pool_w, pool_scale.reshape(1, wpool), ln_g.reshape(1, wsgu),
      ln_b.reshape(1, wsgu), sgu_w, sb)


def _fold_qk_kernel(wq_ref, k_ref, a_ref):
    n_mem = a_ref.shape[2]
    s = lax.dot_general(wq_ref[...].astype(BF16), k_ref[...], (((1,), (1,)), ((), ())),
                        preferred_element_type=F32)
    for b in range(a_ref.shape[0]):
        a_ref[b] = s[:, b * n_mem:(b + 1) * n_mem].astype(a_ref.dtype)


def _fold_vo_kernel(v_ref, wo_ref, b_ref):
    n_mem = b_ref.shape[1]
    s = jnp.dot(v_ref[...], wo_ref[...].astype(BF16), preferred_element_type=F32)
    for b in range(b_ref.shape[0]):
        b_ref[b] = s[b * n_mem:(b + 1) * n_mem].astype(b_ref.dtype)


def _fold_qk(wq, k, *, layer, n_batch, n_mem):
    d = wq.shape[1]
    dh = d // N_XHEADS
    rb = _tile(d, FOLD_TILE)
    return pl.pallas_call(
        _fold_qk_kernel,
        grid=(N_XHEADS, d // rb),
        in_specs=[pl.BlockSpec((None, rb, dh), lambda h, r: (layer, r, h)),
                  pl.BlockSpec((n_batch * n_mem, dh), lambda h, r: (0, h))],
        out_specs=pl.BlockSpec((n_batch, rb, n_mem), lambda h, r: (0, r, h)),
        out_shape=jax.ShapeDtypeStruct((n_batch, d, N_XHEADS * n_mem), BF16),
        compiler_params=_params("parallel", "parallel"),
        name="fold_qk",
    )(wq, k)


def _fold_vo(v, wo, *, layer, n_batch, n_mem):
    d = wo.shape[2]
    dh = d // N_XHEADS
    cb = _tile(d, FOLD_TILE)
    return pl.pallas_call(
        _fold_vo_kernel,
        grid=(N_XHEADS, d // cb),
        in_specs=[pl.BlockSpec((n_batch * n_mem, dh), lambda h, c: (0, h)),
                  pl.BlockSpec((None, dh, cb), lambda h, c: (layer, h, c))],
        out_specs=pl.BlockSpec((n_batch, n_mem, cb), lambda h, c: (0, h, c)),
        out_shape=jax.ShapeDtypeStruct((n_batch, N_XHEADS * n_mem, d), BF16),
        compiler_params=_params("parallel", "parallel"),
        name="fold_vo",
    )(v, wo)


def kernel(x, mem, ln_mix, w_in, pool_w, pool_scale, sgu_ln_g, sgu_ln_b, sgu_w, sgu_b,
           w_out, ln_x, ln_mem, w_q, w_k, w_v, w_o, ln_ffn, w_up, w_down, ln_final):
    n_batch, seq, d = x.shape
    n_mem = mem.shape[1]
    depth = w_in.shape[0]
    dh = d // N_XHEADS
    x = x.reshape(n_batch * seq, d)
    memf = mem.reshape(n_batch * n_mem, d)

    wb_in = w_in[:1].astype(BF16)
    pool_w, sgu_w = pool_w.astype(BF16), sgu_w.astype(BF16)

    blocked = lambda w, l: (w, l, min(MM_BN, w.shape[2]))

    xb, ssq = _prep(x, ln_mix[0])
    for l in range(depth):
        z, (wb_out,) = _mm(xb, wb_in, layer=0 if l == 0 else "blocked", ssq=ssq,
                           casts=[(w_out, l, _mixout_bn(seq, d))], name="mm_in")
        x, xb, ssq, cast = _mixout(z, pool_w, pool_scale[l], sgu_ln_g[l], sgu_ln_b[l], sgu_w,
                                   sgu_b[l], wb_out, x, ln_x[l], layer=l, seq=seq,
                                   cast=blocked(w_up, 0) if l == 0 else None)
        if l == 0:
            wb_up = cast

        mb, mssq = _prep(memf, ln_mem[l])
        k, _ = _mm(mb, w_k, layer=l, ssq=mssq, bn=MM_BN_RES, name="mm_k")
        v, _ = _mm(mb, w_v, layer=l, ssq=mssq, bn=MM_BN_RES, name="mm_v")
        a_fold = _fold_qk(w_q, k, layer=l, n_batch=n_batch, n_mem=n_mem)
        b_fold = _fold_vo(v, w_o, layer=l, n_batch=n_batch, n_mem=n_mem)
        p, _ = _mm(xb, a_fold, ssq=ssq, post_scale=dh ** -0.5, act="group_softmax",
                   n_groups=N_XHEADS, bn=N_XHEADS * n_mem, name="mm_scores")
        (x, xb, ssq), _ = _mm(p, b_fold, res=x, gain=ln_ffn[l], out_dtype=F32,
                              bm=ATTN_OUT_BM, bn=d, name="mm_attn_out")

        last = l + 1 == depth
        nxt = [] if last else [blocked(w_up, l + 1), blocked(w_in, l + 1)]
        hid, cast = _mm(xb, wb_up, layer="blocked", ssq=ssq, act="relu2",
                        casts=[blocked(w_down, l)] + nxt, name="mm_up")
        wb_down = cast[0]
        if last:
            x, _ = _mm(hid, wb_down, layer="blocked", res=x, out_dtype=F32,
                       bk=MM_BK_DOWN_LAST, name="mm_down")
        else:
            wb_up, wb_in = cast[1:]
            (x, xb, ssq), _ = _mm(hid, wb_down, layer="blocked", res=x, gain=ln_mix[l + 1],
                                  out_dtype=F32, bk=MM_BK_DOWN, name="mm_down")

    return _final_norm(x, ln_final).reshape(n_batch, seq, d)
```

```python
import functools

import jax
import jax.numpy as jnp
from jax import lax
from jax.experimental import pallas as pl
from jax.experimental.pallas import tpu as pltpu

EPS = 1e-6
POOL_WINDOWS = (2, 4, 8, 16)
POOL_HALO = 16
CHUNK = 128
SGU_HEAD = 128
N_XHEADS = 4
LANES = 128
BF16_SUBLANES = 16

V7X_VMEM_LIMIT_BYTES = 63 * 1024 * 1024

MM_BM = 1024
MM_BN = 1024
MM_BN_RES = 512
MM_BK = 4096
MM_BK_DOWN = 4096
MM_BK_DOWN_LAST = 4096
ATTN_OUT_BM = 512
MIXOUT_ROWS = 128
ROW_TILE = 256
FOLD_TILE = 1024

F32 = jnp.float32
BF16 = jnp.bfloat16


def _tile(dim, pref):
    t = min(dim, pref)
    assert dim % t == 0, (dim, pref)
    return t


def _params(*sem):
    return pltpu.CompilerParams(dimension_semantics=sem,
                                vmem_limit_bytes=V7X_VMEM_LIMIT_BYTES)


def _fold_lanes(v):
    acc = v[:, :LANES]
    for c in range(1, v.shape[1] // LANES):
        acc = acc + v[:, c * LANES:(c + 1) * LANES]
    return acc


def _row_rsqrt(ssq_ref, inv_width):
    s = ssq_ref[0]
    for c in range(1, ssq_ref.shape[0]):
        s = s + ssq_ref[c]
    return lax.rsqrt(jnp.sum(s, axis=-1, keepdims=True) * inv_width + EPS)


def _prep_kernel(x_ref, g_ref, xb_ref, ssq_ref):
    x = x_ref[...]
    ssq_ref[...] = _fold_lanes(x * x)
    xb_ref[...] = (x * g_ref[...]).astype(BF16)


def _prep(x, g):
    m, d = x.shape
    bm = _tile(m, ROW_TILE)
    return pl.pallas_call(
        _prep_kernel,
        grid=(m // bm,),
        in_specs=[pl.BlockSpec((bm, d), lambda i: (i, 0)),
                  pl.BlockSpec((1, d), lambda i: (0, 0))],
        out_specs=[pl.BlockSpec((bm, d), lambda i: (i, 0)),
                   pl.BlockSpec((None, bm, LANES), lambda i: (0, i, 0))],
        out_shape=[jax.ShapeDtypeStruct((m, d), BF16),
                   jax.ShapeDtypeStruct((1, m, LANES), F32)],
        compiler_params=_params("parallel"),
        name="prep",
    )(x, g.reshape(1, d))


def _final_norm_kernel(x_ref, g_ref, o_ref):
    x = x_ref[...]
    y = x * lax.rsqrt(jnp.mean(x * x, axis=-1, keepdims=True) + EPS)
    o_ref[...] = y * g_ref[...]


def _final_norm(x, g):
    m, d = x.shape
    bm = _tile(m, ROW_TILE)
    return pl.pallas_call(
        _final_norm_kernel,
        grid=(m // bm,),
        in_specs=[pl.BlockSpec((bm, d), lambda i: (i, 0)),
                  pl.BlockSpec((1, d), lambda i: (0, 0))],
        out_specs=pl.BlockSpec((bm, d), lambda i: (i, 0)),
        out_shape=jax.ShapeDtypeStruct((m, d), F32),
        compiler_params=_params("parallel"),
        name="final_norm",
    )(x, g.reshape(1, d))


def _mm_kernel(*refs, nk, has_ssq, has_res, has_gain, n_casts, cast_w, act, n_groups,
               post_scale, inv_k):
    refs = list(refs)
    a_ref, w_ref = refs[0], refs[1]
    pos = 2
    ssq_ref = res_ref = gain_ref = None
    if has_ssq:
        ssq_ref = refs[pos]
        pos += 1
    if has_res:
        res_ref = refs[pos]
        pos += 1
    if has_gain:
        gain_ref = refs[pos]
        pos += 1
    cast_in = refs[pos:pos + n_casts]
    pos += n_casts
    o_ref = refs[pos]
    pos += 1
    xb_ref = ssq_out_ref = None
    if has_gain:
        xb_ref, ssq_out_ref = refs[pos], refs[pos + 1]
        pos += 2
    cast_out = refs[pos:pos + n_casts]
    pos += n_casts
    acc_ref = refs[pos] if nk > 1 else None

    def cast_slabs():
        for ci, co in zip(cast_in, cast_out):
            if len(co.shape) == 2:
                co[...] = ci[...].astype(BF16)
            else:
                cb = co.shape[2]
                for c in range(co.shape[0]):
                    co[c] = ci[:, c * cb:(c + 1) * cb].astype(BF16)

    if nk == 1:
        cast_slabs()
    elif n_casts:
        pl.when(pl.program_id(2) == 0)(cast_slabs)

    def emit(new):
        xb_ref[...] = (new * gain_ref[...]).astype(BF16)
        ssq_out_ref[...] = _fold_lanes(new * new)

    def dot():
        w = w_ref[...]
        if cast_w:
            w = w.astype(BF16)
        return jnp.dot(a_ref[...], w, preferred_element_type=F32)

    if nk > 1:
        assert nk % 2 == 0 and has_res and not has_ssq and act is None and post_scale is None
        k = pl.program_id(2)

        @pl.when(k == 0)
        def _():
            acc_ref[...] = res_ref[...] + dot()

        @pl.when(jnp.logical_and(k > 0, k % 2 == 0))
        def _():
            acc_ref[...] = o_ref[...] + dot()

        @pl.when(jnp.logical_and(k % 2 == 1, k < nk - 1))
        def _():
            o_ref[...] = acc_ref[...] + dot()

        @pl.when(k == nk - 1)
        def _():
            new = acc_ref[...] + dot()
            o_ref[...] = new
            if has_gain:
                emit(new)
        return

    acc = dot()
    if has_ssq:
        acc = acc * _row_rsqrt(ssq_ref, inv_k)
    if post_scale is not None:
        acc = acc * post_scale
    if act == "relu2":
        acc = jnp.square(jnp.maximum(acc, 0.0))
    elif act == "group_softmax":
        gw = acc.shape[1] // n_groups
        parts = []
        for gi in range(n_groups):
            s = acc[:, gi * gw:(gi + 1) * gw]
            e = jnp.exp(s - jnp.max(s, axis=-1, keepdims=True))
            parts.append(e / jnp.sum(e, axis=-1, keepdims=True))
        acc = jnp.concatenate(parts, axis=1)
    if has_res:
        acc = acc + res_ref[...]
    o_ref[...] = acc.astype(o_ref.dtype)
    if has_gain:
        emit(acc)


def _mm(a, w, *, layer=None, ssq=None, post_scale=None, res=None, act=None, n_groups=1,
        gain=None, casts=(), out_dtype=BF16, bm=None, bn=None, bk=None, name="mm"):
    m, kdim = a.shape
    nl, _, n = w.shape
    if layer == "blocked":
        bn, n = n, nl * n
    rows_per_w = m if layer is not None else m // nl
    bm = _tile(rows_per_w, bm or MM_BM)
    bn = _tile(n, bn or MM_BN)
    bk = _tile(kdim, bk or MM_BK)
    nk = kdim // bk
    blocks_per_w = rows_per_w // bm
    if layer == "blocked":
        w_map = lambda i, j, k: (j, k, 0)
    elif layer is not None:
        w_map = lambda i, j, k: (layer, k, j)
    else:
        w_map = lambda i, j, k: (i // blocks_per_w, k, j)

    in_specs = [pl.BlockSpec((bm, bk), lambda i, j, k: (i, k)),
                pl.BlockSpec((None, bk, bn), w_map)]
    args = [a, w]
    if ssq is not None:
        in_specs.append(pl.BlockSpec((ssq.shape[0], bm, LANES), lambda i, j, k: (0, i, 0)))
        args.append(ssq)
    if res is not None:
        in_specs.append(pl.BlockSpec((bm, bn), lambda i, j, k: (i, j)))
        args.append(res)
    out_specs = [pl.BlockSpec((bm, bn), lambda i, j, k: (i, j))]
    out_shape = [jax.ShapeDtypeStruct((m, n), out_dtype)]
    if gain is not None:
        in_specs.append(pl.BlockSpec((1, bn), lambda i, j, k: (0, j)))
        args.append(gain.reshape(1, n))
        mode = dict(pipeline_mode=pl.Buffered(1)) if nk > 1 else {}
        out_specs += [pl.BlockSpec((bm, bn), lambda i, j, k: (i, j), **mode),
                      pl.BlockSpec((None, bm, LANES), lambda i, j, k: (j, i, 0), **mode)]
        out_shape += [jax.ShapeDtypeStruct((m, n), BF16),
                      jax.ShapeDtypeStruct((n // bn, m, LANES), F32)]
    n_main = len(out_shape)
    gi, gj = m // bm, n // bn
    for src, src_layer, *col_block in casts:
        _, kw, nw = src.shape
        cj = max(c for c in range(1, gj + 1)
                 if kw % (gi * c) == 0 and (kw // (gi * c)) % BF16_SUBLANES == 0)
        rows = kw // (gi * cj)
        slab = lambda i, j, k, cj=cj: i * cj + jnp.minimum(j, cj - 1)
        in_specs.append(pl.BlockSpec(
            (None, rows, nw), lambda i, j, k, sl=slab, l=src_layer: (l, sl(i, j, k), 0)))
        args.append(src)
        nblk = nw // col_block[0] if col_block else 1
        first = nblk if col_block else None
        out_specs.append(pl.BlockSpec(
            (first, rows, nw // nblk), lambda i, j, k, sl=slab: (0, sl(i, j, k), 0)))
        out_shape.append(jax.ShapeDtypeStruct((nblk, kw, nw // nblk), BF16))
    kern = functools.partial(
        _mm_kernel, nk=nk, has_ssq=ssq is not None, has_res=res is not None,
        has_gain=gain is not None, n_casts=len(casts), cast_w=w.dtype != BF16, act=act,
        n_groups=n_groups, post_scale=post_scale, inv_k=1.0 / kdim)
    outs = pl.pallas_call(
        kern,
        grid=(gi, gj, nk),
        in_specs=in_specs,
        out_specs=out_specs,
        out_shape=out_shape,
        scratch_shapes=[pltpu.VMEM((bm, bn), F32)] if nk > 1 else [],
        compiler_params=_params("parallel", "arbitrary", "arbitrary"),
        name=name,
    )(*args)
    main =tuple(outs[:n_main]) if gain is not None else outs[0]
    return main, list(outs[n_main:])


def _mix_rows(z_ref, halo_ref, pw_ref, ps_ref, lg_ref, lb_ref, sw_ref, sb_ref, y_ref, v_ref,
              pos0, wpool, wsgu):
    t = z_ref.shape[0]
    n_groups = len(POOL_WINDOWS)
    gc = wpool // n_groups

    halo = halo_ref[...]
    halo = jnp.where(pos0 == 0, jnp.zeros_like(halo), halo)
    row = lax.broadcasted_iota(jnp.int32, (t, t + POOL_HALO), 0)
    col = lax.broadcasted_iota(jnp.int32, (t, t + POOL_HALO), 1)
    lag = row + POOL_HALO - col
    pos = pos0 + lax.broadcasted_iota(jnp.int32, (t, 1), 0)
    for gi, win in enumerate(POOL_WINDOWS):
        cs = slice(gi * gc, (gi + 1) * gc)
        zg = z_ref[:, cs]
        band = jnp.logical_and(lag >= 0, lag < win).astype(BF16)
        wsum = jnp.dot(band, jnp.concatenate([halo[:, cs], zg], axis=0),
                       preferred_element_type=F32)
        cnt = jnp.minimum(pos + 1, win).astype(F32)
        d = wsum / cnt - zg.astype(F32)
        yg = jnp.dot(d.astype(BF16), pw_ref[gi], preferred_element_type=F32)
        y_ref[:, cs] = (yg * ps_ref[:, cs]).astype(BF16)

    gv = jax.nn.gelu(z_ref[:, wpool + wsgu:].astype(F32))
    mu = jnp.mean(gv, axis=-1, keepdims=True)
    cen = gv - mu
    var = jnp.mean(cen * cen, axis=-1, keepdims=True)
    v_ref[...] = (cen * lax.rsqrt(var + EPS) * lg_ref[...] + lb_ref[...]).astype(BF16)

    tri = (lax.broadcasted_iota(jnp.int32, (CHUNK, CHUNK), 0)
           >= lax.broadcasted_iota(jnp.int32, (CHUNK, CHUNK), 1))
    for h in range(wsgu // SGU_HEAD):
        hs = slice(h * SGU_HEAD, (h + 1) * SGU_HEAD)
        us = slice(wpool + h * SGU_HEAD, wpool + (h + 1) * SGU_HEAD)
        wm = jnp.where(tri, sw_ref[h], jnp.zeros((CHUNK, CHUNK), BF16))
        for c in range(t // CHUNK):
            rs = slice(c * CHUNK, (c + 1) * CHUNK)
            mixed = jnp.dot(wm, v_ref[rs, hs], preferred_element_type=F32) + sb_ref[:, hs]
            u = jax.nn.gelu(z_ref[rs, us].astype(F32))
            y_ref[rs, us] = (u * mixed).astype(BF16)


def _mixout_kernel(*refs, seq, wpool, wsgu, n_tiles, has_cast):
    (z_ref, halo_ref, pw_ref, ps_ref, lg_ref, lb_ref, sw_ref, sb_ref,
     w_ref, res_ref, gain_ref, y0_hbm) = refs[:12]
    pos = 12
    cast_in = cast_out = None
    if has_cast:
        cast_in = refs[pos]
        pos += 1
    o_ref, xb_ref, ssq_out_ref = refs[pos:pos + 3]
    pos += 3
    if has_cast:
        cast_out = refs[pos]
        pos += 1
    y_even, y_odd, v_ref, y0_sem = refs[pos:pos + 4]
    i, j = pl.program_id(0), pl.program_id(1)
    bm = y_even.shape[0]

    if has_cast:
        cb = cast_out.shape[2]
        for c in range(cast_out.shape[0]):
            cast_out[c] = cast_in[:, c * cb:(c + 1) * cb].astype(BF16)

    @pl.when(jnp.logical_and(i == 0, j == 0))
    def _():
        y0_copy = pltpu.make_async_copy(y0_hbm, y_even, y0_sem)
        y0_copy.start()
        y0_copy.wait()

    t = z_ref.shape[0]
    row0 = jnp.minimum(i + 1, n_tiles - 1) * bm + j * t

    def step(y_read, y_write):
        chunk = y_write.at[pl.ds(pl.multiple_of(j * t, t), t), :]
        _mix_rows(z_ref, halo_ref, pw_ref, ps_ref, lg_ref, lb_ref, sw_ref, sb_ref, chunk,
                  v_ref, row0 % seq, wpool, wsgu)
        new = jnp.dot(y_read[...], w_ref[...], preferred_element_type=F32) + res_ref[...]
        o_ref[...] = new
        xb_ref[...] = (new * gain_ref[...]).astype(BF16)
        ssq_out_ref[...] = _fold_lanes(new * new)

    @pl.when(i % 2 == 0)
    def _():
        step(y_even, y_odd)

    @pl.when(i % 2 == 1)
    def _():
        step(y_odd, y_even)


def _mixout_bn(seq, n):
    gj = _tile(seq, MM_BM) // _tile(seq, MIXOUT_ROWS)
    assert n % (gj * LANES) == 0
    return n // gj


def _mixout(z, pool_w, pool_scale, ln_g, ln_b, sgu_w, sgu_b, w, res, gain, *, layer, seq,
            cast=None):
    m, d_in = z.shape
    _, n_groups, gc, _ = pool_w.shape
    wpool = n_groups * gc
    wsgu = (d_in - wpool) // 2
    n_heads = wsgu // SGU_HEAD
    assert n_groups == len(POOL_WINDOWS) and sgu_w.shape[1:] == (n_heads, CHUNK, CHUNK)
    gj, kdim, bn = w.shape
    n = gj * bn
    bm = _tile(seq, MM_BM)
    t = bm // gj
    n_tiles = m // bm
    assert kdim == wpool + wsgu and bn == _mixout_bn(seq, n)
    assert t % CHUNK == 0 and t % POOL_HALO == 0
    halo_blocks = t // POOL_HALO
    sb = jnp.repeat(jnp.transpose(sgu_b), SGU_HEAD, axis=1)

    y0 = _mix(z, pool_w, pool_scale, ln_g, ln_b, sgu_w, sgu_b, layer=layer, seq=seq, rows=bm)

    chunk_idx = lambda i, j: jnp.minimum(i + 1, n_tiles - 1) * gj + j
    const2 = lambda i, j: (0, 0)
    once = dict(pipeline_mode=pl.Buffered(1))
    in_specs = [pl.BlockSpec((t, d_in), lambda i, j: (chunk_idx(i, j), 0)),
                pl.BlockSpec((POOL_HALO, wpool),
                             lambda i, j: (jnp.maximum(chunk_idx(i, j) * halo_blocks - 1, 0), 0)),
                pl.BlockSpec((None, n_groups, gc, gc), lambda i, j: (layer, 0, 0, 0), **once),
                pl.BlockSpec((1, wpool), const2, **once),
                pl.BlockSpec((1, wsgu), const2, **once),
                pl.BlockSpec((1, wsgu), const2, **once),
                pl.BlockSpec((None, n_heads, CHUNK, CHUNK), lambda i, j: (layer, 0, 0, 0),
                             **once),
                pl.BlockSpec((CHUNK, wsgu), const2, **once),
                pl.BlockSpec((None, kdim, bn), lambda i, j: (j, 0, 0)),
                pl.BlockSpec((bm, bn), lambda i, j: (i, j)),
                pl.BlockSpec((1, bn), lambda i, j: (0, j)),
                pl.BlockSpec(memory_space=pl.ANY)]
    args = [z, z, pool_w, pool_scale.reshape(1, wpool), ln_g.reshape(1, wsgu),
            ln_b.reshape(1, wsgu), sgu_w, sb, w, res, gain.reshape(1, n), y0]
    out_specs = [pl.BlockSpec((bm, bn), lambda i, j: (i, j)),
                 pl.BlockSpec((bm, bn), lambda i, j: (i, j)),
                 pl.BlockSpec((None, bm, LANES), lambda i, j: (j, i, 0))]
    out_shape = [jax.ShapeDtypeStruct((m, n), F32),
                 jax.ShapeDtypeStruct((m, n), BF16),
                 jax.ShapeDtypeStruct((gj, m, LANES), F32)]
    if cast is not None:
        src, src_layer, cb = cast
        _, kw, nw = src.shape
        steps = n_tiles * gj
        cs = max(c for c in range(1, steps + 1)
                 if kw % c == 0 and (kw // c) % BF16_SUBLANES == 0)
        slab = lambda i, j: jnp.minimum(i * gj + j, cs - 1)
        in_specs.append(pl.BlockSpec((None, kw // cs, nw),
                                     lambda i, j: (src_layer, slab(i, j), 0)))
        args.append(src)
        out_specs.append(pl.BlockSpec((nw // cb, kw // cs, cb),
                                      lambda i, j: (0, slab(i, j), 0)))
        out_shape.append(jax.ShapeDtypeStruct((nw // cb, kw, cb), BF16))
    kern = functools.partial(_mixout_kernel, seq=seq, wpool=wpool, wsgu=wsgu, n_tiles=n_tiles,
                             has_cast=cast is not None)
    outs = pl.pallas_call(
        kern,
        grid=(n_tiles, gj),
        in_specs=in_specs,
        out_specs=out_specs,
        out_shape=out_shape,
        scratch_shapes=[pltpu.VMEM((bm, kdim), BF16), pltpu.VMEM((bm, kdim), BF16),
                        pltpu.VMEM((t, wsgu), BF16), pltpu.SemaphoreType.DMA(())],
        compiler_params=_params("arbitrary", "arbitrary"),
        name="mixout",
    )(*args)
    return outs[0], outs[1], outs[2], (outs[3] if cast is not None else None)


def _mix_kernel(z_ref, halo_ref, pw_ref, ps_ref, lg_ref, lb_ref, sw_ref, sb_ref,
                y_ref, v_ref, *, seq, wpool, wsgu):
    pos0 = (pl.program_id(0) * z_ref.shape[0]) % seq
    _mix_rows(z_ref, halo_ref, pw_ref, ps_ref, lg_ref, lb_ref, sw_ref, sb_ref, y_ref, v_ref,
              pos0, wpool, wsgu)


def _mix(z, pool_w, pool_scale, ln_g, ln_b, sgu_w, sgu_b, *, layer, seq, rows):
    m, d_in = rows, z.shape[1]
    _, n_groups, gc, _ = pool_w.shape
    wpool = n_groups * gc
    wsgu = (d_in - wpool) // 2
    n_heads = wsgu // SGU_HEAD
    assert n_groups == len(POOL_WINDOWS) and sgu_w.shape[1:] == (n_heads, CHUNK, CHUNK)
    t = _tile(seq, ROW_TILE)
    assert t % CHUNK == 0 and t % POOL_HALO == 0
    halo_blocks = t // POOL_HALO
    sb = jnp.repeat(jnp.transpose(sgu_b), SGU_HEAD, axis=1)
    kern = functools.partial(_mix_kernel, seq=seq, wpool=wpool, wsgu=wsgu)
    const2 = lambda i: (0, 0)
    return pl.pallas_call(
        kern,
        grid=(m // t,),
        in_specs=[pl.BlockSpec((t, d_in), lambda i: (i, 0)),
                  pl.BlockSpec((POOL_HALO, wpool),
                               lambda i: (jnp.maximum(i * halo_blocks - 1, 0), 0)),
                  pl.BlockSpec((None, n_groups, gc, gc), lambda i: (layer, 0, 0, 0)),
                  pl.BlockSpec((1, wpool), const2),
                  pl.BlockSpec((1, wsgu), const2),
                  pl.BlockSpec((1, wsgu), const2),
                  pl.BlockSpec((None, n_heads, CHUNK, CHUNK), lambda i: (layer, 0, 0, 0)),
                  pl.BlockSpec((CHUNK, wsgu), const2)],
        out_specs=pl.BlockSpec((t, wpool + wsgu), lambda i: (i, 0)),
        out_shape=jax.ShapeDtypeStruct((m, wpool + wsgu), BF16),
        scratch_shapes=[pltpu.VMEM((t, wsgu), BF16)],
        compiler_params=_params("parallel"),
        name="mix",
    )(z, z, pool_w, pool_scale.reshape(1, wpool), ln_g.reshape(1, wsgu),
      ln_b.reshape(1, wsgu), sgu_w, sb)


def _fold_qk_kernel(wq_ref, k_ref, a_ref):
    n_mem = a_ref.shape[2]
    s = lax.dot_general(wq_ref[...].astype(BF16), k_ref[...], (((1,), (1,)), ((), ())),
                        preferred_element_type=F32)
    for b in range(a_ref.shape[0]):
        a_ref[b] = s[:, b * n_mem:(b + 1) * n_mem].astype(a_ref.dtype)


def _fold_vo_kernel(v_ref, wo_ref, b_ref):
    n_mem = b_ref.shape[1]
    s = jnp.dot(v_ref[...], wo_ref[...].astype(BF16), preferred_element_type=F32)
    for b in range(b_ref.shape[0]):
        b_ref[b] = s[b * n_mem:(b + 1) * n_mem].astype(b_ref.dtype)


def _fold_qk(wq, k, *, layer, n_batch, n_mem):
    d = wq.shape[1]
    dh = d // N_XHEADS
    rb = _tile(d, FOLD_TILE)
    return pl.pallas_call(
        _fold_qk_kernel,
        grid=(N_XHEADS, d // rb),
        in_specs=[pl.BlockSpec((None, rb, dh), lambda h, r: (layer, r, h)),
                  pl.BlockSpec((n_batch * n_mem, dh), lambda h, r: (0, h))],
        out_specs=pl.BlockSpec((n_batch, rb, n_mem), lambda h, r: (0, r, h)),
        out_shape=jax.ShapeDtypeStruct((n_batch, d, N_XHEADS * n_mem), BF16),
        compiler_params=_params("parallel", "parallel"),
        name="fold_qk",
    )(wq, k)


def _fold_vo(v, wo, *, layer, n_batch, n_mem):
    d = wo.shape[2]
    dh = d // N_XHEADS
    cb = _tile(d, FOLD_TILE)
    return pl.pallas_call(
        _fold_vo_kernel,
        grid=(N_XHEADS, d // cb),
        in_specs=[pl.BlockSpec((n_batch * n_mem, dh), lambda h, c: (0, h)),
                  pl.BlockSpec((None, dh, cb), lambda h, c: (layer, h, c))],
        out_specs=pl.BlockSpec((n_batch, n_mem, cb), lambda h, c: (0, h, c)),
        out_shape=jax.ShapeDtypeStruct((n_batch, N_XHEADS * n_mem, d), BF16),
        compiler_params=_params("parallel", "parallel"),
        name="fold_vo",
    )(v, wo)


def kernel(x, mem, ln_mix, w_in, pool_w, pool_scale, sgu_ln_g, sgu_ln_b, sgu_w, sgu_b,
           w_out, ln_x, ln_mem, w_q, w_k, w_v, w_o, ln_ffn, w_up, w_down, ln_final):
    n_batch, seq, d = x.shape
    n_mem = mem.shape[1]
    depth = w_in.shape[0]
    dh = d // N_XHEADS
    x = x.reshape(n_batch * seq, d)
    memf = mem.reshape(n_batch * n_mem, d)

    wb_in = w_in[:1].astype(BF16)
    pool_w, sgu_w = pool_w.astype(BF16), sgu_w.astype(BF16)

    blocked = lambda w, l: (w, l, min(MM_BN, w.shape[2]))

    xb, ssq = _prep(x, ln_mix[0])
    for l in range(depth):
        z, (wb_out,) = _mm(xb, wb_in, layer=0 if l == 0 else "blocked", ssq=ssq,
                           casts=[(w_out, l, _mixout_bn(seq, d))], name="mm_in")
        x, xb, ssq, cast = _mixout(z, pool_w, pool_scale[l], sgu_ln_g[l], sgu_ln_b[l], sgu_w,
                                   sgu_b[l], wb_out, x, ln_x[l], layer=l, seq=seq,
                                   cast=blocked(w_up, 0) if l == 0 else None)
        if l == 0:
            wb_up = cast

        mb, mssq = _prep(memf, ln_mem[l])
        k, _ = _mm(mb, w_k, layer=l, ssq=mssq, bn=MM_BN_RES, name="mm_k")
        v, _ = _mm(mb, w_v, layer=l, ssq=mssq, bn=MM_BN_RES, name="mm_v")
        a_fold = _fold_qk(w_q, k, layer=l, n_batch=n_batch, n_mem=n_mem)
        b_fold = _fold_vo(v, w_o, layer=l, n_batch=n_batch, n_mem=n_mem)
        p, _ = _mm(xb, a_fold, ssq=ssq, post_scale=dh ** -0.5, act="group_softmax",
                   n_groups=N_XHEADS, bn=N_XHEADS * n_mem, name="mm_scores")
        (x, xb, ssq), _ = _mm(p, b_fold, res=x, gain=ln_ffn[l], out_dtype=F32,
                              bm=ATTN_OUT_BM, bn=d, name="mm_attn_out")

        last = l + 1 == depth
        nxt = [] if last else [blocked(w_up, l + 1), blocked(w_in, l + 1)]
        hid, cast = _mm(xb, wb_up, layer="blocked", ssq=ssq, act="relu2",
                        casts=[blocked(w_down, l)] + nxt, name="mm_up")
        wb_down = cast[0]
        if last:
            x, _ = _mm(hid, wb_down, layer="blocked", res=x, out_dtype=F32,
                       bk=MM_BK_DOWN_LAST, name="mm_down")
        else:
            wb_up, wb_in = cast[1:]
            (x, xb, ssq), _ = _mm(hid, wb_down, layer="blocked", res=x, gain=ln_mix[l + 1],
                                  out_dtype=F32, bk=MM_BK_DOWN, name="mm_down")

    return _final_norm(x, ln_final).reshape(n_batch, seq, d)
```

```python
import functools

import jax
import jax.numpy as jnp
from jax import lax
from jax.experimental import pallas as pl
from jax.experimental.pallas import tpu as pltpu

EPS = 1e-6
POOL_WINDOWS = (2, 4, 8, 16)
POOL_HALO = 16
CHUNK = 128
SGU_HEAD = 128
N_XHEADS = 4
LANES = 128
BF16_SUBLANES = 16

V7X_VMEM_LIMIT_BYTES = 63 * 1024 * 1024

MM_BM = 1024
MM_BN = 1024
MM_BN_RES = 512
MM_BK = 4096
MM_BK_DOWN = 4096
ATTN_OUT_BM = 512
MIXOUT_ROWS = 128
ROW_TILE = 256
FOLD_TILE = 1024

F32 = jnp.float32
BF16 = jnp.bfloat16


def _tile(dim, pref):
    t = min(dim, pref)
    assert dim % t == 0, (dim, pref)
    return t


def _params(*sem):
    return pltpu.CompilerParams(dimension_semantics=sem,
                                vmem_limit_bytes=V7X_VMEM_LIMIT_BYTES)


def _fold_lanes(v):
    acc = v[:, :LANES]
    for c in range(1, v.shape[1] // LANES):
        acc = acc + v[:, c * LANES:(c + 1) * LANES]
    return acc


def _row_rsqrt(ssq_ref, inv_width):
    s = ssq_ref[0]
    for c in range(1, ssq_ref.shape[0]):
        s = s + ssq_ref[c]
    return lax.rsqrt(jnp.sum(s, axis=-1, keepdims=True) * inv_width + EPS)


def _prep_kernel(x_ref, g_ref, xb_ref, ssq_ref):
    x = x_ref[...]
    ssq_ref[...] = _fold_lanes(x * x)
    xb_ref[...] = (x * g_ref[...]).astype(BF16)


def _prep(x, g, rows=None):
    m, d = rows or x.shape[0], x.shape[1]
    bm = _tile(m, ROW_TILE)
    return pl.pallas_call(
        _prep_kernel,
        grid=(m // bm,),
        in_specs=[pl.BlockSpec((bm, d), lambda i: (i, 0)),
                  pl.BlockSpec((1, d), lambda i: (0, 0))],
        out_specs=[pl.BlockSpec((bm, d), lambda i: (i, 0)),
                   pl.BlockSpec((None, bm, LANES), lambda i: (0, i, 0))],
        out_shape=[jax.ShapeDtypeStruct((m, d), BF16),
                   jax.ShapeDtypeStruct((1, m, LANES), F32)],
        compiler_params=_params("parallel"),
        name="prep",
    )(x, g.reshape(1, d))


def _final_norm_kernel(x_ref, g_ref, o_ref):
    x = x_ref[...]
    y = x * lax.rsqrt(jnp.mean(x * x, axis=-1, keepdims=True) + EPS)
    o_ref[...] = y * g_ref[...]


def _final_norm(x, g):
    m, d = x.shape
    bm = _tile(m, ROW_TILE)
    return pl.pallas_call(
        _final_norm_kernel,
        grid=(m // bm,),
        in_specs=[pl.BlockSpec((bm, d), lambda i: (i, 0)),
                  pl.BlockSpec((1, d), lambda i: (0, 0))],
        out_specs=pl.BlockSpec((bm, d), lambda i: (i, 0)),
        out_shape=jax.ShapeDtypeStruct((m, d), F32),
        compiler_params=_params("parallel"),
        name="final_norm",
    )(x, g.reshape(1, d))


def _mm_kernel(*refs, nk, has_ssq, has_res, has_gain, n_casts, cast_w, act, n_groups,
               post_scale, inv_k):
    refs = list(refs)
    a_ref, w_ref = refs[0], refs[1]
    pos = 2
    ssq_ref = res_ref = gain_ref = None
    if has_ssq:
        ssq_ref = refs[pos]
        pos += 1
    if has_res:
        res_ref = refs[pos]
        pos += 1
    if has_gain:
        gain_ref = refs[pos]
        pos += 1
    cast_in = refs[pos:pos + n_casts]
    pos += n_casts
    o_ref = refs[pos]
    pos += 1
    xb_ref = ssq_out_ref = None
    if has_gain:
        xb_ref, ssq_out_ref = refs[pos], refs[pos + 1]
        pos += 2
    cast_out = refs[pos:pos + n_casts]
    pos += n_casts
    acc_ref = refs[pos] if nk > 1 else None

    def cast_slabs():
        for ci, co in zip(cast_in, cast_out):
            if len(co.shape) == 2:
                co[...] = ci[...].astype(BF16)
            else:
                cb = co.shape[2]
                for c in range(co.shape[0]):
                    co[c] = ci[:, c * cb:(c + 1) * cb].astype(BF16)

    if nk == 1:
        cast_slabs()
    elif n_casts:
        pl.when(pl.program_id(2) == 0)(cast_slabs)

    def emit(new):
        xb_ref[...] = (new * gain_ref[...]).astype(BF16)
        ssq_out_ref[...] = _fold_lanes(new * new)

    def dot():
        w = w_ref[...]
        if cast_w:
            w = w.astype(BF16)
        return jnp.dot(a_ref[...], w, preferred_element_type=F32)

    if nk > 1:
        assert nk % 2 == 0 and has_res and not has_ssq and act is None and post_scale is None
        k = pl.program_id(2)

        @pl.when(k == 0)
        def _():
            acc_ref[...] = res_ref[...] + dot()

        @pl.when(jnp.logical_and(k > 0, k % 2 == 0))
        def _():
            acc_ref[...] = o_ref[...] + dot()

        @pl.when(jnp.logical_and(k % 2 == 1, k < nk - 1))
        def _():
            o_ref[...] = acc_ref[...] + dot()

        @pl.when(k == nk - 1)
        def _():
            new = acc_ref[...] + dot()
            o_ref[...] = new
            if has_gain:
                emit(new)
        return

    acc = dot()
    if has_ssq:
        acc = acc * _row_rsqrt(ssq_ref, inv_k)
    if post_scale is not None:
        acc = acc * post_scale
    if act == "relu2":
        acc = jnp.square(jnp.maximum(acc, 0.0))
    elif act == "group_softmax":
        gw = acc.shape[1] // n_groups
        parts = []
        for gi in range(n_groups):
            s = acc[:, gi * gw:(gi + 1) * gw]
            e = jnp.exp(s - jnp.max(s, axis=-1, keepdims=True))
            parts.append(e / jnp.sum(e, axis=-1, keepdims=True))
        acc = jnp.concatenate(parts, axis=1)
    if has_res:
        acc = acc + res_ref[...]
    o_ref[...] = acc.astype(o_ref.dtype)
    if has_gain:
        emit(acc)


def _mm(a, w, *, layer=None, ssq=None, post_scale=None, res=None, act=None, n_groups=1,
        gain=None, casts=(), out_dtype=BF16, bm=None, bn=None, bk=None, name="mm"):
    m, kdim = a.shape
    nl, _, n = w.shape
    if layer == "blocked":
        bn, n = n, nl * n
    rows_per_w = m if layer is not None else m // nl
    bm = _tile(rows_per_w, bm or MM_BM)
    bn = _tile(n, bn or MM_BN)
    bk = _tile(kdim, bk or MM_BK)
    nk = kdim // bk
    blocks_per_w = rows_per_w // bm
    if layer == "blocked":
        w_map = lambda i, j, k: (j, k, 0)
    elif layer is not None:
        w_map = lambda i, j, k: (layer, k, j)
    else:
        w_map = lambda i, j, k: (i // blocks_per_w, k, j)

    in_specs = [pl.BlockSpec((bm, bk), lambda i, j, k: (i, k)),
                pl.BlockSpec((None, bk, bn), w_map)]
    args = [a, w]
    if ssq is not None:
        in_specs.append(pl.BlockSpec((ssq.shape[0], bm, LANES), lambda i, j, k: (0, i, 0)))
        args.append(ssq)
    if res is not None:
        in_specs.append(pl.BlockSpec((bm, bn), lambda i, j, k: (i, j)))
        args.append(res)
    out_specs = [pl.BlockSpec((bm, bn), lambda i, j, k: (i, j))]
    out_shape = [jax.ShapeDtypeStruct((m, n), out_dtype)]
    if gain is not None:
        in_specs.append(pl.BlockSpec((1, bn), lambda i, j, k: (0, j)))
        args.append(gain.reshape(1, n))
        mode = dict(pipeline_mode=pl.Buffered(1)) if nk > 1 else {}
        out_specs += [pl.BlockSpec((bm, bn), lambda i, j, k: (i, j), **mode),
                      pl.BlockSpec((None, bm, LANES), lambda i, j, k: (j, i, 0), **mode)]
        out_shape += [jax.ShapeDtypeStruct((m, n), BF16),
                      jax.ShapeDtypeStruct((n // bn, m, LANES), F32)]
    n_main = len(out_shape)
    gi, gj = m // bm, n // bn
    for src, src_layer, *col_block in casts:
        _, kw, nw = src.shape
        cj = max(c for c in range(1, gj + 1)
                 if kw % (gi * c) == 0 and (kw // (gi * c)) % BF16_SUBLANES == 0)
        rows = kw // (gi * cj)
        slab = lambda i, j, k, cj=cj: i * cj + jnp.minimum(j, cj - 1)
        in_specs.append(pl.BlockSpec(
            (None, rows, nw), lambda i, j, k, sl=slab, l=src_layer: (l, sl(i, j, k), 0)))
        args.append(src)
        nblk = nw // col_block[0] if col_block else 1
        first = nblk if col_block else None
        out_specs.append(pl.BlockSpec(
            (first, rows, nw // nblk), lambda i, j, k, sl=slab: (0, sl(i, j, k), 0)))
        out_shape.append(jax.ShapeDtypeStruct((nblk, kw, nw // nblk), BF16))
    kern = functools.partial(
        _mm_kernel, nk=nk, has_ssq=ssq is not None, has_res=res is not None,
        has_gain=gain is not None, n_casts=len(casts), cast_w=w.dtype != BF16, act=act,
        n_groups=n_groups, post_scale=post_scale, inv_k=1.0 / kdim)
    outs = pl.pallas_call(
        kern,
        grid=(gi, gj, nk),
        in_specs=in_specs,
        out_specs=out_specs,
        out_shape=out_shape,
        scratch_shapes=[pltpu.VMEM((bm, bn), F32)] if nk > 1 else [],
        compiler_params=_params("parallel", "arbitrary", "arbitrary"),
        name=name,
    )(*args)
    main =tuple(outs[:n_main]) if gain is not None else outs[0]
    return main, list(outs[n_main:])


def _normin_kernel(*refs, n_tiles, n_chunks, n_casts, inv_d):
    x_ref, g_ref, w_ref, ssq0_ref, xb0_hbm = refs[:5]
    cast_in = refs[5:5 + n_casts]
    o_ref = refs[5 + n_casts]
    cast_out = refs[6 + n_casts:6 + 2 * n_casts]
    xb_even, xb_odd, r_even, r_odd, xb0_sem = refs[6 + 2 * n_casts:]
    i, j = pl.program_id(0), pl.program_id(1)
    ct = x_ref.shape[0]

    for ci, co in zip(cast_in, cast_out):
        cb = co.shape[2]
        for c in range(co.shape[0]):
            co[c] = ci[:, c * cb:(c + 1) * cb].astype(BF16)

    @pl.when(jnp.logical_and(i == 0, j == 0))
    def _():
        xb0_copy = pltpu.make_async_copy(xb0_hbm, xb_even, xb0_sem)
        xb0_copy.start()
        r_even[...] = _row_rsqrt(ssq0_ref, inv_d)
        xb0_copy.wait()

    rows = pl.ds(pl.multiple_of(jnp.minimum(j, n_chunks - 1) * ct, ct), ct)

    def step(xb_read, r_read, xb_write, r_write):
        x = x_ref[...]
        r_write[rows, :] = lax.rsqrt(jnp.sum(x * x, axis=-1, keepdims=True) * inv_d + EPS)
        xb_write[rows, :] = (x * g_ref[...]).astype(BF16)
        acc = jnp.dot(xb_read[...], w_ref[...], preferred_element_type=F32)
        o_ref[...] = (acc * r_read[...]).astype(o_ref.dtype)

    @pl.when(i % 2 == 0)
    def _():
        step(xb_even, r_even, xb_odd, r_odd)

    @pl.when(i % 2 == 1)
    def _():
        step(xb_odd, r_odd, xb_even, r_even)


def _normin(x, g, w, *, layer, casts=()):
    m, d = x.shape
    if layer == "blocked":
        bn, n = w.shape[2], w.shape[0] * w.shape[2]
        w_map = lambda i, j: (j, 0, 0)
    else:
        n = w.shape[2]
        bn = _tile(n, MM_BN)
        w_map = lambda i, j: (0, 0, j)
    bm = _tile(m, MM_BM)
    n_tiles, gj = m // bm, n // bn
    n_chunks = max(c for c in range(1, gj + 1) if bm % (c * BF16_SUBLANES) == 0)
    ct = bm // n_chunks
    xb0, ssq0 = _prep(x, g, rows=bm)

    chunk = lambda i, j: jnp.minimum(i + 1, n_tiles - 1) * n_chunks + jnp.minimum(j, n_chunks - 1)
    in_specs = [pl.BlockSpec((ct, d), lambda i, j: (chunk(i, j), 0)),
                pl.BlockSpec((1, d), lambda i, j: (0, 0), pipeline_mode=pl.Buffered(1)),
                pl.BlockSpec((None, d, bn), w_map),
                pl.BlockSpec((1, bm, LANES), lambda i, j: (0, 0, 0),
                             pipeline_mode=pl.Buffered(1)),
                pl.BlockSpec(memory_space=pl.ANY)]
    args = [x, g.reshape(1, d), w, ssq0, xb0]
    out_specs = [pl.BlockSpec((bm, bn), lambda i, j: (i, j))]
    out_shape = [jax.ShapeDtypeStruct((m, n), BF16)]
    for src, src_layer, cb in casts:
        _, kw, nw = src.shape
        cs = max(c for c in range(1, n_tiles * gj + 1)
                 if kw % c == 0 and (kw // c) % BF16_SUBLANES == 0)
        slab = lambda i, j, cs=cs: jnp.minimum(i * gj + j, cs - 1)
        in_specs.append(pl.BlockSpec((None, kw // cs, nw),
                                     lambda i, j, sl=slab, l=src_layer: (l, sl(i, j), 0)))
        args.append(src)
        out_specs.append(pl.BlockSpec((nw // cb, kw // cs, cb),
                                      lambda i, j, sl=slab: (0, sl(i, j), 0)))
        out_shape.append(jax.ShapeDtypeStruct((nw // cb, kw, cb), BF16))
    kern = functools.partial(_normin_kernel, n_tiles=n_tiles, n_chunks=n_chunks,
                             n_casts=len(casts), inv_d=1.0 / d)
    outs = pl.pallas_call(
        kern,
        grid=(n_tiles, gj),
        in_specs=in_specs,
        out_specs=out_specs,
        out_shape=out_shape,
        scratch_shapes=[pltpu.VMEM((bm, d), BF16), pltpu.VMEM((bm, d), BF16),
                        pltpu.VMEM((bm, 1), F32), pltpu.VMEM((bm, 1), F32),
                        pltpu.SemaphoreType.DMA(())],
        compiler_params=_params("arbitrary", "arbitrary"),
        name="normin",
    )(*args)
    return outs[0], list(outs[1:])


def _mix_rows(z_ref, halo_ref, pw_ref, ps_ref, lg_ref, lb_ref, sw_ref, sb_ref, y_ref, v_ref,
              pos0, wpool, wsgu):
    t = z_ref.shape[0]
    n_groups = len(POOL_WINDOWS)
    gc = wpool // n_groups

    halo = halo_ref[...]
    halo = jnp.where(pos0 == 0, jnp.zeros_like(halo), halo)
    row = lax.broadcasted_iota(jnp.int32, (t, t + POOL_HALO), 0)
    col = lax.broadcasted_iota(jnp.int32, (t, t + POOL_HALO), 1)
    lag = row + POOL_HALO - col
    pos = pos0 + lax.broadcasted_iota(jnp.int32, (t, 1), 0)
    for gi, win in enumerate(POOL_WINDOWS):
        cs = slice(gi * gc, (gi + 1) * gc)
        zg = z_ref[:, cs]
        band = jnp.logical_and(lag >= 0, lag < win).astype(BF16)
        wsum = jnp.dot(band, jnp.concatenate([halo[:, cs], zg], axis=0),
                       preferred_element_type=F32)
        cnt = jnp.minimum(pos + 1, win).astype(F32)
        d = wsum / cnt - zg.astype(F32)
        yg = jnp.dot(d.astype(BF16), pw_ref[gi], preferred_element_type=F32)
        y_ref[:, cs] = (yg * ps_ref[:, cs]).astype(BF16)

    gv = jax.nn.gelu(z_ref[:, wpool + wsgu:].astype(F32))
    mu = jnp.mean(gv, axis=-1, keepdims=True)
    cen = gv - mu
    var = jnp.mean(cen * cen, axis=-1, keepdims=True)
    v_ref[...] = (cen * lax.rsqrt(var + EPS) * lg_ref[...] + lb_ref[...]).astype(BF16)

    tri = (lax.broadcasted_iota(jnp.int32, (CHUNK, CHUNK), 0)
           >= lax.broadcasted_iota(jnp.int32, (CHUNK, CHUNK), 1))
    for h in range(wsgu // SGU_HEAD):
        hs = slice(h * SGU_HEAD, (h + 1) * SGU_HEAD)
        us = slice(wpool + h * SGU_HEAD, wpool + (h + 1) * SGU_HEAD)
        wm = jnp.where(tri, sw_ref[h], jnp.zeros((CHUNK, CHUNK), BF16))
        for c in range(t // CHUNK):
            rs = slice(c * CHUNK, (c + 1) * CHUNK)
            mixed = jnp.dot(wm, v_ref[rs, hs], preferred_element_type=F32) + sb_ref[:, hs]
            u = jax.nn.gelu(z_ref[rs, us].astype(F32))
            y_ref[rs, us] = (u * mixed).astype(BF16)


def _mixout_kernel(*refs, seq, wpool, wsgu, n_tiles, has_cast):
    (z_ref, halo_ref, pw_ref, ps_ref, lg_ref, lb_ref, sw_ref, sb_ref,
     w_ref, res_ref, gain_ref, y0_hbm) = refs[:12]
    pos = 12
    cast_in = cast_out = None
    if has_cast:
        cast_in = refs[pos]
        pos += 1
    o_ref, xb_ref, ssq_out_ref = refs[pos:pos + 3]
    pos += 3
    if has_cast:
        cast_out = refs[pos]
        pos += 1
    y_even, y_odd, v_ref, y0_sem = refs[pos:pos + 4]
    i, j = pl.program_id(0), pl.program_id(1)
    bm = y_even.shape[0]

    if has_cast:
        cb = cast_out.shape[2]
        for c in range(cast_out.shape[0]):
            cast_out[c] = cast_in[:, c * cb:(c + 1) * cb].astype(BF16)

    @pl.when(jnp.logical_and(i == 0, j == 0))
    def _():
        y0_copy = pltpu.make_async_copy(y0_hbm, y_even, y0_sem)
        y0_copy.start()
        y0_copy.wait()

    t = z_ref.shape[0]
    row0 = jnp.minimum(i + 1, n_tiles - 1) * bm + j * t

    def step(y_read, y_write):
        chunk = y_write.at[pl.ds(pl.multiple_of(j * t, t), t), :]
        _mix_rows(z_ref, halo_ref, pw_ref, ps_ref, lg_ref, lb_ref, sw_ref, sb_ref, chunk,
                  v_ref, row0 % seq, wpool, wsgu)
        new = jnp.dot(y_read[...], w_ref[...], preferred_element_type=F32) + res_ref[...]
        o_ref[...] = new
        xb_ref[...] = (new * gain_ref[...]).astype(BF16)
        ssq_out_ref[...] = _fold_lanes(new * new)

    @pl.when(i % 2 == 0)
    def _():
        step(y_even, y_odd)

    @pl.when(i % 2 == 1)
    def _():
        step(y_odd, y_even)


def _mixout_bn(seq, n):
    gj = _tile(seq, MM_BM) // _tile(seq, MIXOUT_ROWS)
    assert n % (gj * LANES) == 0
    return n // gj


def _mixout(z, pool_w, pool_scale, ln_g, ln_b, sgu_w, sgu_b, w, res, gain, *, layer, seq,
            cast=None):
    m, d_in = z.shape
    _, n_groups, gc, _ = pool_w.shape
    wpool = n_groups * gc
    wsgu = (d_in - wpool) // 2
    n_heads = wsgu // SGU_HEAD
    assert n_groups == len(POOL_WINDOWS) and sgu_w.shape[1:] == (n_heads, CHUNK, CHUNK)
    gj, kdim, bn = w.shape
    n = gj * bn
    bm = _tile(seq, MM_BM)
    t = bm // gj
    n_tiles = m // bm
    assert kdim == wpool + wsgu and bn == _mixout_bn(seq, n)
    assert t % CHUNK == 0 and t % POOL_HALO == 0
    halo_blocks = t // POOL_HALO
    sb = jnp.repeat(jnp.transpose(sgu_b), SGU_HEAD, axis=1)

    y0 = _mix(z, pool_w, pool_scale, ln_g, ln_b, sgu_w, sgu_b, layer=layer, seq=seq, rows=bm)

    chunk_idx = lambda i, j: jnp.minimum(i + 1, n_tiles - 1) * gj + j
    const2 = lambda i, j: (0, 0)
    once = dict(pipeline_mode=pl.Buffered(1))
    in_specs = [pl.BlockSpec((t, d_in), lambda i, j: (chunk_idx(i, j), 0)),
                pl.BlockSpec((POOL_HALO, wpool),
                             lambda i, j: (jnp.maximum(chunk_idx(i, j) * halo_blocks - 1, 0), 0)),
                pl.BlockSpec((None, n_groups, gc, gc), lambda i, j: (layer, 0, 0, 0), **once),
                pl.BlockSpec((1, wpool), const2, **once),
                pl.BlockSpec((1, wsgu), const2, **once),
                pl.BlockSpec((1, wsgu), const2, **once),
                pl.BlockSpec((None, n_heads, CHUNK, CHUNK), lambda i, j: (layer, 0, 0, 0),
                             **once),
                pl.BlockSpec((CHUNK, wsgu), const2, **once),
                pl.BlockSpec((None, kdim, bn), lambda i, j: (j, 0, 0)),
                pl.BlockSpec((bm, bn), lambda i, j: (i, j)),
                pl.BlockSpec((1, bn), lambda i, j: (0, j)),
                pl.BlockSpec(memory_space=pl.ANY)]
    args = [z, z, pool_w, pool_scale.reshape(1, wpool), ln_g.reshape(1, wsgu),
            ln_b.reshape(1, wsgu), sgu_w, sb, w, res, gain.reshape(1, n), y0]
    out_specs = [pl.BlockSpec((bm, bn), lambda i, j: (i, j)),
                 pl.BlockSpec((bm, bn), lambda i, j: (i, j)),
                 pl.BlockSpec((None, bm, LANES), lambda i, j: (j, i, 0))]
    out_shape = [jax.ShapeDtypeStruct((m, n), F32),
                 jax.ShapeDtypeStruct((m, n), BF16),
                 jax.ShapeDtypeStruct((gj, m, LANES), F32)]
    if cast is not None:
        src, src_layer, cb = cast
        _, kw, nw = src.shape
        steps = n_tiles * gj
        cs = max(c for c in range(1, steps + 1)
                 if kw % c == 0 and (kw // c) % BF16_SUBLANES == 0)
        slab = lambda i, j: jnp.minimum(i * gj + j, cs - 1)
        in_specs.append(pl.BlockSpec((None, kw // cs, nw),
                                     lambda i, j: (src_layer, slab(i, j), 0)))
        args.append(src)
        out_specs.append(pl.BlockSpec((nw // cb, kw // cs, cb),
                                      lambda i, j: (0, slab(i, j), 0)))
        out_shape.append(jax.ShapeDtypeStruct((nw // cb, kw, cb), BF16))
    kern = functools.partial(_mixout_kernel, seq=seq, wpool=wpool, wsgu=wsgu, n_tiles=n_tiles,
                             has_cast=cast is not None)
    outs = pl.pallas_call(
        kern,
        grid=(n_tiles, gj),
        in_specs=in_specs,
        out_specs=out_specs,
        out_shape=out_shape,
        scratch_shapes=[pltpu.VMEM((bm, kdim), BF16), pltpu.VMEM((bm, kdim), BF16),
                        pltpu.VMEM((t, wsgu), BF16), pltpu.SemaphoreType.DMA(())],
        compiler_params=_params("arbitrary", "arbitrary"),
        name="mixout",
    )(*args)
    return outs[0], outs[1], outs[2], (outs[3] if cast is not None else None)


def _mix_kernel(z_ref, halo_ref, pw_ref, ps_ref, lg_ref, lb_ref, sw_ref, sb_ref,
                y_ref, v_ref, *, seq, wpool, wsgu):
    pos0 = (pl.program_id(0) * z_ref.shape[0]) % seq
    _mix_rows(z_ref, halo_ref, pw_ref, ps_ref, lg_ref, lb_ref, sw_ref, sb_ref, y_ref, v_ref,
              pos0, wpool, wsgu)


def _mix(z, pool_w, pool_scale, ln_g, ln_b, sgu_w, sgu_b, *, layer, seq, rows):
    m, d_in = rows, z.shape[1]
    _, n_groups, gc, _ = pool_w.shape
    wpool = n_groups * gc
    wsgu = (d_in - wpool) // 2
    n_heads = wsgu // SGU_HEAD
    assert n_groups == len(POOL_WINDOWS) and sgu_w.shape[1:] == (n_heads, CHUNK, CHUNK)
    t = _tile(seq, ROW_TILE)
    assert t % CHUNK == 0 and t % POOL_HALO == 0
    halo_blocks = t // POOL_HALO
    sb = jnp.repeat(jnp.transpose(sgu_b), SGU_HEAD, axis=1)
    kern = functools.partial(_mix_kernel, seq=seq, wpool=wpool, wsgu=wsgu)
    const2 = lambda i: (0, 0)
    return pl.pallas_call(
        kern,
        grid=(m // t,),
        in_specs=[pl.BlockSpec((t, d_in), lambda i: (i, 0)),
                  pl.BlockSpec((POOL_HALO, wpool),
                               lambda i: (jnp.maximum(i * halo_blocks - 1, 0), 0)),
                  pl.BlockSpec((None, n_groups, gc, gc), lambda i: (layer, 0, 0, 0)),
                  pl.BlockSpec((1, wpool), const2),
                  pl.BlockSpec((1, wsgu), const2),
                  pl.BlockSpec((1, wsgu), const2),
                  pl.BlockSpec((None, n_heads, CHUNK, CHUNK), lambda i: (layer, 0, 0, 0)),
                  pl.BlockSpec((CHUNK, wsgu), const2)],
        out_specs=pl.BlockSpec((t, wpool + wsgu), lambda i: (i, 0)),
        out_shape=jax.ShapeDtypeStruct((m, wpool + wsgu), BF16),
        scratch_shapes=[pltpu.VMEM((t, wsgu), BF16)],
        compiler_params=_params("parallel"),
        name="mix",
    )(z, z, pool_w, pool_scale.reshape(1, wpool), ln_g.reshape(1, wsgu),
      ln_b.reshape(1, wsgu), sgu_w, sb)


def _fold_qk_kernel(wq_ref, k_ref, a_ref):
    n_mem = a_ref.shape[2]
    s = lax.dot_general(wq_ref[...].astype(BF16), k_ref[...], (((1,), (1,)), ((), ())),
                        preferred_element_type=F32)
    for b in range(a_ref.shape[0]):
        a_ref[b] = s[:, b * n_mem:(b + 1) * n_mem].astype(a_ref.dtype)


def _fold_vo_kernel(v_ref, wo_ref, b_ref):
    n_mem = b_ref.shape[1]
    s = jnp.dot(v_ref[...], wo_ref[...].astype(BF16), preferred_element_type=F32)
    for b in range(b_ref.shape[0]):
        b_ref[b] = s[b * n_mem:(b + 1) * n_mem].astype(b_ref.dtype)


def _fold_qk(wq, k, *, layer, n_batch, n_mem):
    d = wq.shape[1]
    dh = d // N_XHEADS
    rb = _tile(d, FOLD_TILE)
    return pl.pallas_call(
        _fold_qk_kernel,
        grid=(N_XHEADS, d // rb),
        in_specs=[pl.BlockSpec((None, rb, dh), lambda h, r: (layer, r, h)),
                  pl.BlockSpec((n_batch * n_mem, dh), lambda h, r: (0, h))],
        out_specs=pl.BlockSpec((n_batch, rb, n_mem), lambda h, r: (0, r, h)),
        out_shape=jax.ShapeDtypeStruct((n_batch, d, N_XHEADS * n_mem), BF16),
        compiler_params=_params("parallel", "parallel"),
        name="fold_qk",
    )(wq, k)


def _fold_vo(v, wo, *, layer, n_batch, n_mem):
    d = wo.shape[2]
    dh = d // N_XHEADS
    cb = _tile(d, FOLD_TILE)
    return pl.pallas_call(
        _fold_vo_kernel,
        grid=(N_XHEADS, d // cb),
        in_specs=[pl.BlockSpec((n_batch * n_mem, dh), lambda h, c: (0, h)),
                  pl.BlockSpec((None, dh, cb), lambda h, c: (layer, h, c))],
        out_specs=pl.BlockSpec((n_batch, n_mem, cb), lambda h, c: (0, h, c)),
        out_shape=jax.ShapeDtypeStruct((n_batch, N_XHEADS * n_mem, d), BF16),
        compiler_params=_params("parallel", "parallel"),
        name="fold_vo",
    )(v, wo)


def kernel(x, mem, ln_mix, w_in, pool_w, pool_scale, sgu_ln_g, sgu_ln_b, sgu_w, sgu_b,
           w_out, ln_x, ln_mem, w_q, w_k, w_v, w_o, ln_ffn, w_up, w_down, ln_final):
    n_batch, seq, d = x.shape
    n_mem = mem.shape[1]
    depth = w_in.shape[0]
    dh = d // N_XHEADS
    x = x.reshape(n_batch * seq, d)
    memf = mem.reshape(n_batch * n_mem, d)

    wb_in = w_in[:1].astype(BF16)
    pool_w, sgu_w = pool_w.astype(BF16), sgu_w.astype(BF16)

    blocked = lambda w, l: (w, l, min(MM_BN, w.shape[2]))

    for l in range(depth):
        z, (wb_out,) = _normin(x, ln_mix[l], wb_in, layer=0 if l == 0 else "blocked",
                               casts=[(w_out, l, _mixout_bn(seq, d))])
        x, xb, ssq, cast = _mixout(z, pool_w, pool_scale[l], sgu_ln_g[l], sgu_ln_b[l], sgu_w,
                                   sgu_b[l], wb_out, x, ln_x[l], layer=l, seq=seq,
                                   cast=blocked(w_up, 0) if l == 0 else None)
        if l == 0:
            wb_up = cast

        mb, mssq = _prep(memf, ln_mem[l])
        k, _ = _mm(mb, w_k, layer=l, ssq=mssq, bn=MM_BN_RES, name="mm_k")
        v, _ = _mm(mb, w_v, layer=l, ssq=mssq, bn=MM_BN_RES, name="mm_v")
        a_fold = _fold_qk(w_q, k, layer=l, n_batch=n_batch, n_mem=n_mem)
        b_fold = _fold_vo(v, w_o, layer=l, n_batch=n_batch, n_mem=n_mem)
        p, _ = _mm(xb, a_fold, ssq=ssq, post_scale=dh ** -0.5, act="group_softmax",
                   n_groups=N_XHEADS, bn=N_XHEADS * n_mem, name="mm_scores")
        (x, xb, ssq), _ = _mm(p, b_fold, res=x, gain=ln_ffn[l], out_dtype=F32,
                              bm=ATTN_OUT_BM, bn=d, name="mm_attn_out")

        last = l + 1 == depth
        nxt = [] if last else [blocked(w_up, l + 1), blocked(w_in, l + 1)]
        hid, cast = _mm(xb, wb_up, layer="blocked", ssq=ssq, act="relu2",
                        casts=[blocked(w_down, l)] + nxt, name="mm_up")
        wb_down = cast[0]
        if not last:
            wb_up, wb_in = cast[1:]
        x, _ = _mm(hid, wb_down, layer="blocked", res=x, out_dtype=F32, bk=MM_BK_DOWN,
                   name="mm_down")

    return _final_norm(x, ln_final).reshape(n_batch, seq, d)
```

```python
import functools

import jax
import jax.numpy as jnp
from jax import lax
from jax.experimental import pallas as pl
from jax.experimental.pallas import tpu as pltpu

EPS = 1e-6
POOL_WINDOWS = (2, 4, 8, 16)
POOL_HALO = 16
CHUNK = 128
SGU_HEAD = 128
N_XHEADS = 4
LANES = 128
BF16_SUBLANES = 16

V7X_VMEM_LIMIT_BYTES = 63 * 1024 * 1024

MM_BM = 1024
MM_BN = 1024
MM_BN_RES = 512
MM_BK = 4096
MM_BK_DOWN = 4096
ATTN_OUT_BM = 512
MIXOUT_ROWS = 128
ROW_TILE = 256
FOLD_TILE = 1024

F32 = jnp.float32
BF16 = jnp.bfloat16


def _tile(dim, pref):
    t = min(dim, pref)
    assert dim % t == 0, (dim, pref)
    return t


def _params(*sem):
    return pltpu.CompilerParams(dimension_semantics=sem,
                                vmem_limit_bytes=V7X_VMEM_LIMIT_BYTES)


def _fold_lanes(v):
    acc = v[:, :LANES]
    for c in range(1, v.shape[1] // LANES):
        acc = acc + v[:, c * LANES:(c + 1) * LANES]
    return acc


def _row_rsqrt(ssq_ref, inv_width):
    s = ssq_ref[0]
    for c in range(1, ssq_ref.shape[0]):
        s = s + ssq_ref[c]
    return lax.rsqrt(jnp.sum(s, axis=-1, keepdims=True) * inv_width + EPS)


def _prep_kernel(x_ref, g_ref, xb_ref, ssq_ref):
    x = x_ref[...]
    ssq_ref[...] = _fold_lanes(x * x)
    xb_ref[...] = (x * g_ref[...]).astype(BF16)


def _prep(x, g, rows=None):
    m, d = rows or x.shape[0], x.shape[1]
    bm = _tile(m, ROW_TILE)
    return pl.pallas_call(
        _prep_kernel,
        grid=(m // bm,),
        in_specs=[pl.BlockSpec((bm, d), lambda i: (i, 0)),
                  pl.BlockSpec((1, d), lambda i: (0, 0))],
        out_specs=[pl.BlockSpec((bm, d), lambda i: (i, 0)),
                   pl.BlockSpec((None, bm, LANES), lambda i: (0, i, 0))],
        out_shape=[jax.ShapeDtypeStruct((m, d), BF16),
                   jax.ShapeDtypeStruct((1, m, LANES), F32)],
        compiler_params=_params("parallel"),
        name="prep",
    )(x, g.reshape(1, d))


def _final_norm_kernel(x_ref, g_ref, o_ref):
    x = x_ref[...]
    y = x * lax.rsqrt(jnp.mean(x * x, axis=-1, keepdims=True) + EPS)
    o_ref[...] = y * g_ref[...]


def _final_norm(x, g):
    m, d = x.shape
    bm = _tile(m, ROW_TILE)
    return pl.pallas_call(
        _final_norm_kernel,
        grid=(m // bm,),
        in_specs=[pl.BlockSpec((bm, d), lambda i: (i, 0)),
                  pl.BlockSpec((1, d), lambda i: (0, 0))],
        out_specs=pl.BlockSpec((bm, d), lambda i: (i, 0)),
        out_shape=jax.ShapeDtypeStruct((m, d), F32),
        compiler_params=_params("parallel"),
        name="final_norm",
    )(x, g.reshape(1, d))


def _mm_kernel(*refs, nk, has_ssq, has_res, has_gain, n_casts, cast_w, act, n_groups,
               post_scale, inv_k):
    refs = list(refs)
    a_ref, w_ref = refs[0], refs[1]
    pos = 2
    ssq_ref = res_ref = gain_ref = None
    if has_ssq:
        ssq_ref = refs[pos]
        pos += 1
    if has_res:
        res_ref = refs[pos]
        pos += 1
    if has_gain:
        gain_ref = refs[pos]
        pos += 1
    cast_in = refs[pos:pos + n_casts]
    pos += n_casts
    o_ref = refs[pos]
    pos += 1
    xb_ref = ssq_out_ref = None
    if has_gain:
        xb_ref, ssq_out_ref = refs[pos], refs[pos + 1]
        pos += 2
    cast_out = refs[pos:pos + n_casts]
    pos += n_casts
    acc_ref = refs[pos] if nk > 1 else None

    def cast_slabs():
        for ci, co in zip(cast_in, cast_out):
            if len(co.shape) == 2:
                co[...] = ci[...].astype(BF16)
            else:
                cb = co.shape[2]
                for c in range(co.shape[0]):
                    co[c] = ci[:, c * cb:(c + 1) * cb].astype(BF16)

    if nk == 1:
        cast_slabs()
    elif n_casts:
        pl.when(pl.program_id(2) == 0)(cast_slabs)

    def emit(new):
        xb_ref[...] = (new * gain_ref[...]).astype(BF16)
        ssq_out_ref[...] = _fold_lanes(new * new)

    def dot():
        w = w_ref[...]
        if cast_w:
            w = w.astype(BF16)
        return jnp.dot(a_ref[...], w, preferred_element_type=F32)

    if nk > 1:
        assert nk % 2 == 0 and has_res and not has_ssq and act is None and post_scale is None
        k = pl.program_id(2)

        @pl.when(k == 0)
        def _():
            acc_ref[...] = res_ref[...] + dot()

        @pl.when(jnp.logical_and(k > 0, k % 2 == 0))
        def _():
            acc_ref[...] = o_ref[...] + dot()

        @pl.when(jnp.logical_and(k % 2 == 1, k < nk - 1))
        def _():
            o_ref[...] = acc_ref[...] + dot()

        @pl.when(k == nk - 1)
        def _():
            new = acc_ref[...] + dot()
            o_ref[...] = new
            if has_gain:
                emit(new)
        return

    acc = dot()
    if has_ssq:
        acc = acc * _row_rsqrt(ssq_ref, inv_k)
    if post_scale is not None:
        acc = acc * post_scale
    if act == "relu2":
        acc = jnp.square(jnp.maximum(acc, 0.0))
    elif act == "group_softmax":
        gw = acc.shape[1] // n_groups
        parts = []
        for gi in range(n_groups):
            s = acc[:, gi * gw:(gi + 1) * gw]
            e = jnp.exp(s - jnp.max(s, axis=-1, keepdims=True))
            parts.append(e / jnp.sum(e, axis=-1, keepdims=True))
        acc = jnp.concatenate(parts, axis=1)
    if has_res:
        acc = acc + res_ref[...]
    o_ref[...] = acc.astype(o_ref.dtype)
    if has_gain:
        emit(acc)


def _mm(a, w, *, layer=None, ssq=None, post_scale=None, res=None, act=None, n_groups=1,
        gain=None, casts=(), out_dtype=BF16, bm=None, bn=None, bk=None, name="mm"):
    m, kdim = a.shape
    nl, _, n = w.shape
    if layer == "blocked":
        bn, n = n, nl * n
    rows_per_w = m if layer is not None else m // nl
    bm = _tile(rows_per_w, bm or MM_BM)
    bn = _tile(n, bn or MM_BN)
    bk = _tile(kdim, bk or MM_BK)
    nk = kdim // bk
    blocks_per_w = rows_per_w // bm
    if layer == "blocked":
        w_map = lambda i, j, k: (j, k, 0)
    elif layer is not None:
        w_map = lambda i, j, k: (layer, k, j)
    else:
        w_map = lambda i, j, k: (i // blocks_per_w, k, j)

    in_specs = [pl.BlockSpec((bm, bk), lambda i, j, k: (i, k)),
                pl.BlockSpec((None, bk, bn), w_map)]
    args = [a, w]
    if ssq is not None:
        in_specs.append(pl.BlockSpec((ssq.shape[0], bm, LANES), lambda i, j, k: (0, i, 0)))
        args.append(ssq)
    if res is not None:
        in_specs.append(pl.BlockSpec((bm, bn), lambda i, j, k: (i, j)))
        args.append(res)
    out_specs = [pl.BlockSpec((bm, bn), lambda i, j, k: (i, j))]
    out_shape = [jax.ShapeDtypeStruct((m, n), out_dtype)]
    if gain is not None:
        in_specs.append(pl.BlockSpec((1, bn), lambda i, j, k: (0, j)))
        args.append(gain.reshape(1, n))
        mode = dict(pipeline_mode=pl.Buffered(1)) if nk > 1 else {}
        out_specs += [pl.BlockSpec((bm, bn), lambda i, j, k: (i, j), **mode),
                      pl.BlockSpec((None, bm, LANES), lambda i, j, k: (j, i, 0), **mode)]
        out_shape += [jax.ShapeDtypeStruct((m, n), BF16),
                      jax.ShapeDtypeStruct((n // bn, m, LANES), F32)]
    n_main = len(out_shape)
    gi, gj = m // bm, n // bn
    for src, src_layer, *col_block in casts:
        _, kw, nw = src.shape
        cj = max(c for c in range(1, gj + 1)
                 if kw % (gi * c) == 0 and (kw // (gi * c)) % BF16_SUBLANES == 0)
        rows = kw // (gi * cj)
        slab = lambda i, j, k, cj=cj: i * cj + jnp.minimum(j, cj - 1)
        in_specs.append(pl.BlockSpec(
            (None, rows, nw), lambda i, j, k, sl=slab, l=src_layer: (l, sl(i, j, k), 0)))
        args.append(src)
        nblk = nw // col_block[0] if col_block else 1
        first = nblk if col_block else None
        out_specs.append(pl.BlockSpec(
            (first, rows, nw // nblk), lambda i, j, k, sl=slab: (0, sl(i, j, k), 0)))
        out_shape.append(jax.ShapeDtypeStruct((nblk, kw, nw // nblk), BF16))
    kern = functools.partial(
        _mm_kernel, nk=nk, has_ssq=ssq is not None, has_res=res is not None,
        has_gain=gain is not None, n_casts=len(casts), cast_w=w.dtype != BF16, act=act,
        n_groups=n_groups, post_scale=post_scale, inv_k=1.0 / kdim)
    outs = pl.pallas_call(
        kern,
        grid=(gi, gj, nk),
        in_specs=in_specs,
        out_specs=out_specs,
        out_shape=out_shape,
        scratch_shapes=[pltpu.VMEM((bm, bn), F32)] if nk > 1 else [],
        compiler_params=_params("parallel", "arbitrary", "arbitrary"),
        name=name,
    )(*args)
    main =tuple(outs[:n_main]) if gain is not None else outs[0]
    return main, list(outs[n_main:])


def _normin_kernel(*refs, n_tiles, n_chunks, n_casts, inv_d, act):
    x_ref, g_ref, w_ref, ssq0_ref, xb0_hbm = refs[:5]
    cast_in = refs[5:5 + n_casts]
    o_ref = refs[5 + n_casts]
    cast_out = refs[6 + n_casts:6 + 2 * n_casts]
    xb_even, xb_odd, r_even, r_odd, xb0_sem = refs[6 + 2 * n_casts:]
    i, j = pl.program_id(0), pl.program_id(1)
    ct = x_ref.shape[0]

    @pl.when(jnp.logical_and(i == 0, j == 0))
    def _():
        xb0_copy = pltpu.make_async_copy(xb0_hbm, xb_even, xb0_sem)
        xb0_copy.start()
        r_even[...] = _row_rsqrt(ssq0_ref, inv_d)
        xb0_copy.wait()

    rows = pl.ds(pl.multiple_of(jnp.minimum(j, n_chunks - 1) * ct, ct), ct)

    def step(xb_read, r_read, xb_write, r_write):
        for ci, co in zip(cast_in, cast_out):
            cb = co.shape[2]
            for c in range(co.shape[0]):
                co[c] = ci[:, c * cb:(c + 1) * cb].astype(BF16)
        x = x_ref[...]
        r_write[rows, :] = lax.rsqrt(jnp.sum(x * x, axis=-1, keepdims=True) * inv_d + EPS)
        xb_write[rows, :] = (x * g_ref[...]).astype(BF16)
        acc = jnp.dot(xb_read[...], w_ref[...], preferred_element_type=F32)
        acc = acc * r_read[...]
        if act == "relu2":
            acc = jnp.square(jnp.maximum(acc, 0.0))
        o_ref[...] = acc.astype(o_ref.dtype)

    @pl.when(i % 2 == 0)
    def _():
        step(xb_even, r_even, xb_odd, r_odd)

    @pl.when(i % 2 == 1)
    def _():
        step(xb_odd, r_odd, xb_even, r_even)


def _normin(x, g, w, *, layer, act=None, casts=(), name="normin"):
    m, d = x.shape
    if layer == "blocked":
        bn, n = w.shape[2], w.shape[0] * w.shape[2]
        w_map = lambda i, j: (j, 0, 0)
    else:
        n = w.shape[2]
        bn = _tile(n, MM_BN)
        w_map = lambda i, j: (0, 0, j)
    bm = _tile(m, MM_BM)
    n_tiles, gj = m // bm, n // bn
    n_chunks = max(c for c in range(1, gj + 1) if bm % (c * BF16_SUBLANES) == 0)
    ct = bm // n_chunks
    xb0, ssq0 = _prep(x, g, rows=bm)

    chunk = lambda i, j: jnp.minimum(i + 1, n_tiles - 1) * n_chunks + jnp.minimum(j, n_chunks - 1)
    in_specs = [pl.BlockSpec((ct, d), lambda i, j: (chunk(i, j), 0)),
                pl.BlockSpec((1, d), lambda i, j: (0, 0), pipeline_mode=pl.Buffered(1)),
                pl.BlockSpec((None, d, bn), w_map),
                pl.BlockSpec((1, bm, LANES), lambda i, j: (0, 0, 0),
                             pipeline_mode=pl.Buffered(1)),
                pl.BlockSpec(memory_space=pl.ANY)]
    args = [x, g.reshape(1, d), w, ssq0, xb0]
    out_specs = [pl.BlockSpec((bm, bn), lambda i, j: (i, j))]
    out_shape = [jax.ShapeDtypeStruct((m, n), BF16)]
    for src, src_layer, cb in casts:
        _, kw, nw = src.shape
        cs = max(c for c in range(1, n_tiles * gj + 1)
                 if kw % c == 0 and (kw // c) % BF16_SUBLANES == 0)
        slab = lambda i, j, cs=cs: jnp.minimum(i * gj + j, cs - 1)
        in_specs.append(pl.BlockSpec((None, kw // cs, nw),
                                     lambda i, j, sl=slab, l=src_layer: (l, sl(i, j), 0)))
        args.append(src)
        out_specs.append(pl.BlockSpec((nw // cb, kw // cs, cb),
                                      lambda i, j, sl=slab: (0, sl(i, j), 0)))
        out_shape.append(jax.ShapeDtypeStruct((nw // cb, kw, cb), BF16))
    kern = functools.partial(_normin_kernel, n_tiles=n_tiles, n_chunks=n_chunks,
                             n_casts=len(casts), inv_d=1.0 / d, act=act)
    outs = pl.pallas_call(
        kern,
        grid=(n_tiles, gj),
        in_specs=in_specs,
        out_specs=out_specs,
        out_shape=out_shape,
        scratch_shapes=[pltpu.VMEM((bm, d), BF16), pltpu.VMEM((bm, d), BF16),
                        pltpu.VMEM((bm, 1), F32), pltpu.VMEM((bm, 1), F32),
                        pltpu.SemaphoreType.DMA(())],
        compiler_params=_params("arbitrary", "arbitrary"),
        name=name,
    )(*args)
    return outs[0], list(outs[1:])


def _mix_rows(z_ref, halo_ref, pw_ref, ps_ref, lg_ref, lb_ref, sw_ref, sb_ref, y_ref, v_ref,
              pos0, wpool, wsgu):
    t = z_ref.shape[0]
    n_groups = len(POOL_WINDOWS)
    gc = wpool // n_groups

    halo = halo_ref[...]
    halo = jnp.where(pos0 == 0, jnp.zeros_like(halo), halo)
    row = lax.broadcasted_iota(jnp.int32, (t, t + POOL_HALO), 0)
    col = lax.broadcasted_iota(jnp.int32, (t, t + POOL_HALO), 1)
    lag = row + POOL_HALO - col
    pos = pos0 + lax.broadcasted_iota(jnp.int32, (t, 1), 0)
    for gi, win in enumerate(POOL_WINDOWS):
        cs = slice(gi * gc, (gi + 1) * gc)
        zg = z_ref[:, cs]
        band = jnp.logical_and(lag >= 0, lag < win).astype(BF16)
        wsum = jnp.dot(band, jnp.concatenate([halo[:, cs], zg], axis=0),
                       preferred_element_type=F32)
        cnt = jnp.minimum(pos + 1, win).astype(F32)
        d = wsum / cnt - zg.astype(F32)
        yg = jnp.dot(d.astype(BF16), pw_ref[gi], preferred_element_type=F32)
        y_ref[:, cs] = (yg * ps_ref[:, cs]).astype(BF16)

    gv = jax.nn.gelu(z_ref[:, wpool + wsgu:].astype(F32))
    mu = jnp.mean(gv, axis=-1, keepdims=True)
    cen = gv - mu
    var = jnp.mean(cen * cen, axis=-1, keepdims=True)
    v_ref[...] = (cen * lax.rsqrt(var + EPS) * lg_ref[...] + lb_ref[...]).astype(BF16)

    tri = (lax.broadcasted_iota(jnp.int32, (CHUNK, CHUNK), 0)
           >= lax.broadcasted_iota(jnp.int32, (CHUNK, CHUNK), 1))
    for h in range(wsgu // SGU_HEAD):
        hs = slice(h * SGU_HEAD, (h + 1) * SGU_HEAD)
        us = slice(wpool + h * SGU_HEAD, wpool + (h + 1) * SGU_HEAD)
        wm = jnp.where(tri, sw_ref[h], jnp.zeros((CHUNK, CHUNK), BF16))
        for c in range(t // CHUNK):
            rs = slice(c * CHUNK, (c + 1) * CHUNK)
            mixed = jnp.dot(wm, v_ref[rs, hs], preferred_element_type=F32) + sb_ref[:, hs]
            u = jax.nn.gelu(z_ref[rs, us].astype(F32))
            y_ref[rs, us] = (u * mixed).astype(BF16)


def _mixout_kernel(*refs, seq, wpool, wsgu, n_tiles, has_cast):
    (z_ref, halo_ref, pw_ref, ps_ref, lg_ref, lb_ref, sw_ref, sb_ref,
     w_ref, res_ref, gain_ref, y0_hbm) = refs[:12]
    pos = 12
    cast_in = cast_out = None
    if has_cast:
        cast_in = refs[pos]
        pos += 1
    o_ref, xb_ref, ssq_out_ref = refs[pos:pos + 3]
    pos += 3
    if has_cast:
        cast_out = refs[pos]
        pos += 1
    y_even, y_odd, v_ref, y0_sem = refs[pos:pos + 4]
    i, j = pl.program_id(0), pl.program_id(1)
    bm = y_even.shape[0]

    @pl.when(jnp.logical_and(i == 0, j == 0))
    def _():
        y0_copy = pltpu.make_async_copy(y0_hbm, y_even, y0_sem)
        y0_copy.start()
        y0_copy.wait()

    t = z_ref.shape[0]
    row0 = jnp.minimum(i + 1, n_tiles - 1) * bm + j * t

    def step(y_read, y_write):
        if has_cast:
            cb = cast_out.shape[2]
            for c in range(cast_out.shape[0]):
                cast_out[c] = cast_in[:, c * cb:(c + 1) * cb].astype(BF16)
        chunk = y_write.at[pl.ds(pl.multiple_of(j * t, t), t), :]
        _mix_rows(z_ref, halo_ref, pw_ref, ps_ref, lg_ref, lb_ref, sw_ref, sb_ref, chunk,
                  v_ref, row0 % seq, wpool, wsgu)
        new = jnp.dot(y_read[...], w_ref[...], preferred_element_type=F32) + res_ref[...]
        o_ref[...] = new
        xb_ref[...] = (new * gain_ref[...]).astype(BF16)
        ssq_out_ref[...] = _fold_lanes(new * new)

    @pl.when(i % 2 == 0)
    def _():
        step(y_even, y_odd)

    @pl.when(i % 2 == 1)
    def _():
        step(y_odd, y_even)


def _mixout_bn(seq, n):
    gj = _tile(seq, MM_BM) // _tile(seq, MIXOUT_ROWS)
    assert n % (gj * LANES) == 0
    return n // gj


def _mixout(z, pool_w, pool_scale, ln_g, ln_b, sgu_w, sgu_b, w, res, gain, *, layer, seq,
            cast=None):
    m, d_in = z.shape
    _, n_groups, gc, _ = pool_w.shape
    wpool = n_groups * gc
    wsgu = (d_in - wpool) // 2
    n_heads = wsgu // SGU_HEAD
    assert n_groups == len(POOL_WINDOWS) and sgu_w.shape[1:] == (n_heads, CHUNK, CHUNK)
    gj, kdim, bn = w.shape
    n = gj * bn
    bm = _tile(seq, MM_BM)
    t = bm // gj
    n_tiles = m // bm
    assert kdim == wpool + wsgu and bn == _mixout_bn(seq, n)
    assert t % CHUNK == 0 and t % POOL_HALO == 0
    halo_blocks = t // POOL_HALO
    sb = jnp.repeat(jnp.transpose(sgu_b), SGU_HEAD, axis=1)

    y0 = _mix(z, pool_w, pool_scale, ln_g, ln_b, sgu_w, sgu_b, layer=layer, seq=seq, rows=bm)

    chunk_idx = lambda i, j: jnp.minimum(i + 1, n_tiles - 1) * gj + j
    const2 = lambda i, j: (0, 0)
    once = dict(pipeline_mode=pl.Buffered(1))
    in_specs = [pl.BlockSpec((t, d_in), lambda i, j: (chunk_idx(i, j), 0)),
                pl.BlockSpec((POOL_HALO, wpool),
                             lambda i, j: (jnp.maximum(chunk_idx(i, j) * halo_blocks - 1, 0), 0)),
                pl.BlockSpec((None, n_groups, gc, gc), lambda i, j: (layer, 0, 0, 0), **once),
                pl.BlockSpec((1, wpool), const2, **once),
                pl.BlockSpec((1, wsgu), const2, **once),
                pl.BlockSpec((1, wsgu), const2, **once),
                pl.BlockSpec((None, n_heads, CHUNK, CHUNK), lambda i, j: (layer, 0, 0, 0),
                             **once),
                pl.BlockSpec((CHUNK, wsgu), const2, **once),
                pl.BlockSpec((None, kdim, bn), lambda i, j: (j, 0, 0)),
                pl.BlockSpec((bm, bn), lambda i, j: (i, j)),
                pl.BlockSpec((1, bn), lambda i, j: (0, j)),
                pl.BlockSpec(memory_space=pl.ANY)]
    args = [z, z, pool_w, pool_scale.reshape(1, wpool), ln_g.reshape(1, wsgu),
            ln_b.reshape(1, wsgu), sgu_w, sb, w, res, gain.reshape(1, n), y0]
    out_specs = [pl.BlockSpec((bm, bn), lambda i, j: (i, j)),
                 pl.BlockSpec((bm, bn), lambda i, j: (i, j)),
                 pl.BlockSpec((None, bm, LANES), lambda i, j: (j, i, 0))]
    out_shape = [jax.ShapeDtypeStruct((m, n), F32),
                 jax.ShapeDtypeStruct((m, n), BF16),
                 jax.ShapeDtypeStruct((gj, m, LANES), F32)]
    if cast is not None:
        src, src_layer, cb = cast
        _, kw, nw = src.shape
        steps = n_tiles * gj
        cs = max(c for c in range(1, steps + 1)
                 if kw % c == 0 and (kw // c) % BF16_SUBLANES == 0)
        slab = lambda i, j: jnp.minimum(i * gj + j, cs - 1)
        in_specs.append(pl.BlockSpec((None, kw // cs, nw),
                                     lambda i, j: (src_layer, slab(i, j), 0)))
        args.append(src)
        out_specs.append(pl.BlockSpec((nw // cb, kw // cs, cb),
                                      lambda i, j: (0, slab(i, j), 0)))
        out_shape.append(jax.ShapeDtypeStruct((nw // cb, kw, cb), BF16))
    kern = functools.partial(_mixout_kernel, seq=seq, wpool=wpool, wsgu=wsgu, n_tiles=n_tiles,
                             has_cast=cast is not None)
    outs = pl.pallas_call(
        kern,
        grid=(n_tiles, gj),
        in_specs=in_specs,
        out_specs=out_specs,
        out_shape=out_shape,
        scratch_shapes=[pltpu.VMEM((bm, kdim), BF16), pltpu.VMEM((bm, kdim), BF16),
                        pltpu.VMEM((t, wsgu), BF16), pltpu.SemaphoreType.DMA(())],
        compiler_params=_params("arbitrary", "arbitrary"),
        name="mixout",
    )(*args)
    return outs[0], outs[1], outs[2], (outs[3] if cast is not None else None)


def _mix_kernel(z_ref, halo_ref, pw_ref, ps_ref, lg_ref, lb_ref, sw_ref, sb_ref,
                y_ref, v_ref, *, seq, wpool, wsgu):
    pos0 = (pl.program_id(0) * z_ref.shape[0]) % seq
    _mix_rows(z_ref, halo_ref, pw_ref, ps_ref, lg_ref, lb_ref, sw_ref, sb_ref, y_ref, v_ref,
              pos0, wpool, wsgu)


def _mix(z, pool_w, pool_scale, ln_g, ln_b, sgu_w, sgu_b, *, layer, seq, rows):
    m, d_in = rows, z.shape[1]
    _, n_groups, gc, _ = pool_w.shape
    wpool = n_groups * gc
    wsgu = (d_in - wpool) // 2
    n_heads = wsgu // SGU_HEAD
    assert n_groups == len(POOL_WINDOWS) and sgu_w.shape[1:] == (n_heads, CHUNK, CHUNK)
    t = _tile(seq, ROW_TILE)
    assert t % CHUNK == 0 and t % POOL_HALO == 0
    halo_blocks = t // POOL_HALO
    sb = jnp.repeat(jnp.transpose(sgu_b), SGU_HEAD, axis=1)
    kern = functools.partial(_mix_kernel, seq=seq, wpool=wpool, wsgu=wsgu)
    const2 = lambda i: (0, 0)
    return pl.pallas_call(
        kern,
        grid=(m // t,),
        in_specs=[pl.BlockSpec((t, d_in), lambda i: (i, 0)),
                  pl.BlockSpec((POOL_HALO, wpool),
                               lambda i: (jnp.maximum(i * halo_blocks - 1, 0), 0)),
                  pl.BlockSpec((None, n_groups, gc, gc), lambda i: (layer, 0, 0, 0)),
                  pl.BlockSpec((1, wpool), const2),
                  pl.BlockSpec((1, wsgu), const2),
                  pl.BlockSpec((1, wsgu), const2),
                  pl.BlockSpec((None, n_heads, CHUNK, CHUNK), lambda i: (layer, 0, 0, 0)),
                  pl.BlockSpec((CHUNK, wsgu), const2)],
        out_specs=pl.BlockSpec((t, wpool + wsgu), lambda i: (i, 0)),
        out_shape=jax.ShapeDtypeStruct((m, wpool + wsgu), BF16),
        scratch_shapes=[pltpu.VMEM((t, wsgu), BF16)],
        compiler_params=_params("parallel"),
        name="mix",
    )(z, z, pool_w, pool_scale.reshape(1, wpool), ln_g.reshape(1, wsgu),
      ln_b.reshape(1, wsgu), sgu_w, sb)


def _fold_qk_kernel(wq_ref, k_ref, a_ref):
    n_mem = a_ref.shape[2]
    s = lax.dot_general(wq_ref[...].astype(BF16), k_ref[...], (((1,), (1,)), ((), ())),
                        preferred_element_type=F32)
    for b in range(a_ref.shape[0]):
        a_ref[b] = s[:, b * n_mem:(b + 1) * n_mem].astype(a_ref.dtype)


def _fold_vo_kernel(v_ref, wo_ref, b_ref):
    n_mem = b_ref.shape[1]
    s = jnp.dot(v_ref[...], wo_ref[...].astype(BF16), preferred_element_type=F32)
    for b in range(b_ref.shape[0]):
        b_ref[b] = s[b * n_mem:(b + 1) * n_mem].astype(b_ref.dtype)


def _fold_qk(wq, k, *, layer, n_batch, n_mem):
    d = wq.shape[1]
    dh = d // N_XHEADS
    rb = _tile(d, FOLD_TILE)
    return pl.pallas_call(
        _fold_qk_kernel,
        grid=(N_XHEADS, d // rb),
        in_specs=[pl.BlockSpec((None, rb, dh), lambda h, r: (layer, r, h)),
                  pl.BlockSpec((n_batch * n_mem, dh), lambda h, r: (0, h))],
        out_specs=pl.BlockSpec((n_batch, rb, n_mem), lambda h, r: (0, r, h)),
        out_shape=jax.ShapeDtypeStruct((n_batch, d, N_XHEADS * n_mem), BF16),
        compiler_params=_params("parallel", "parallel"),
        name="fold_qk",
    )(wq, k)


def _fold_vo(v, wo, *, layer, n_batch, n_mem):
    d = wo.shape[2]
    dh = d // N_XHEADS
    cb = _tile(d, FOLD_TILE)
    return pl.pallas_call(
        _fold_vo_kernel,
        grid=(N_XHEADS, d // cb),
        in_specs=[pl.BlockSpec((n_batch * n_mem, dh), lambda h, c: (0, h)),
                  pl.BlockSpec((None, dh, cb), lambda h, c: (layer, h, c))],
        out_specs=pl.BlockSpec((n_batch, n_mem, cb), lambda h, c: (0, h, c)),
        out_shape=jax.ShapeDtypeStruct((n_batch, N_XHEADS * n_mem, d), BF16),
        compiler_params=_params("parallel", "parallel"),
        name="fold_vo",
    )(v, wo)


def kernel(x, mem, ln_mix, w_in, pool_w, pool_scale, sgu_ln_g, sgu_ln_b, sgu_w, sgu_b,
           w_out, ln_x, ln_mem, w_q, w_k, w_v, w_o, ln_ffn, w_up, w_down, ln_final):
    n_batch, seq, d = x.shape
    n_mem = mem.shape[1]
    depth = w_in.shape[0]
    dh = d // N_XHEADS
    x = x.reshape(n_batch * seq, d)
    memf = mem.reshape(n_batch * n_mem, d)

    wb_in = w_in[:1].astype(BF16)
    pool_w, sgu_w = pool_w.astype(BF16), sgu_w.astype(BF16)

    blocked = lambda w, l: (w, l, min(MM_BN, w.shape[2]))

    for l in range(depth):
        z, (wb_out,) = _normin(x, ln_mix[l], wb_in, layer=0 if l == 0 else "blocked",
                               casts=[(w_out, l, _mixout_bn(seq, d))])
        x, xb, ssq, cast = _mixout(z, pool_w, pool_scale[l], sgu_ln_g[l], sgu_ln_b[l], sgu_w,
                                   sgu_b[l], wb_out, x, ln_x[l], layer=l, seq=seq,
                                   cast=blocked(w_up, 0) if l == 0 else None)
        if l == 0:
            wb_up = cast

        mb, mssq = _prep(memf, ln_mem[l])
        k, _ = _mm(mb, w_k, layer=l, ssq=mssq, bn=MM_BN_RES, name="mm_k")
        v, _ = _mm(mb, w_v, layer=l, ssq=mssq, bn=MM_BN_RES, name="mm_v")
        a_fold = _fold_qk(w_q, k, layer=l, n_batch=n_batch, n_mem=n_mem)
        b_fold = _fold_vo(v, w_o, layer=l, n_batch=n_batch, n_mem=n_mem)
        p, _ = _mm(xb, a_fold, ssq=ssq, post_scale=dh ** -0.5, act="group_softmax",
                   n_groups=N_XHEADS, bn=N_XHEADS * n_mem, name="mm_scores")
        x, _ = _mm(p, b_fold, res=x, out_dtype=F32, bm=ATTN_OUT_BM, bn=d, name="mm_attn_out")

        last = l + 1 == depth
        nxt = [] if last else [blocked(w_up, l + 1), blocked(w_in, l + 1)]
        hid, cast = _normin(x, ln_ffn[l], wb_up, layer="blocked", act="relu2",
                            casts=[blocked(w_down, l)] + nxt, name="normup")
        wb_down = cast[0]
        if not last:
            wb_up, wb_in = cast[1:]
        x, _ = _mm(hid, wb_down, layer="blocked", res=x, out_dtype=F32, bk=MM_BK_DOWN,
                   name="mm_down")

    return _final_norm(x, ln_final).reshape(n_batch, seq, d)
```

```python
import functools

import jax
import jax.numpy as jnp
from jax import lax
from jax.experimental import pallas as pl
from jax.experimental.pallas import tpu as pltpu

EPS = 1e-6
POOL_WINDOWS = (2, 4, 8, 16)
POOL_HALO = 16
CHUNK = 128
SGU_HEAD = 128
N_XHEADS = 4
LANES = 128
BF16_SUBLANES = 16

V7X_VMEM_LIMIT_BYTES = 63 * 1024 * 1024

MM_BM = 1024
MM_BN = 1024
MM_BN_RES = 512
MM_BK = 4096
MM_BK_DOWN = 4096
ATTN_OUT_BM = 512
MIXOUT_ROWS = 128
RING = 3
ROW_TILE = 256
FOLD_TILE = 1024

F32 = jnp.float32
BF16 = jnp.bfloat16


def _tile(dim, pref):
    t = min(dim, pref)
    assert dim % t == 0, (dim, pref)
    return t


def _params(*sem):
    return pltpu.CompilerParams(dimension_semantics=sem,
                                vmem_limit_bytes=V7X_VMEM_LIMIT_BYTES)


def _fold_lanes(v):
    acc = v[:, :LANES]
    for c in range(1, v.shape[1] // LANES):
        acc = acc + v[:, c * LANES:(c + 1) * LANES]
    return acc


def _row_rsqrt(ssq_ref, inv_width):
    s = ssq_ref[0]
    for c in range(1, ssq_ref.shape[0]):
        s = s + ssq_ref[c]
    return lax.rsqrt(jnp.sum(s, axis=-1, keepdims=True) * inv_width + EPS)


def _prep_kernel(x_ref, g_ref, xb_ref, ssq_ref):
    x = x_ref[...]
    ssq_ref[...] = _fold_lanes(x * x)
    xb_ref[...] = (x * g_ref[...]).astype(BF16)


def _prep(x, g, rows=None):
    m, d = rows or x.shape[0], x.shape[1]
    bm = _tile(m, ROW_TILE)
    return pl.pallas_call(
        _prep_kernel,
        grid=(m // bm,),
        in_specs=[pl.BlockSpec((bm, d), lambda i: (i, 0)),
                  pl.BlockSpec((1, d), lambda i: (0, 0))],
        out_specs=[pl.BlockSpec((bm, d), lambda i: (i, 0)),
                   pl.BlockSpec((None, bm, LANES), lambda i: (0, i, 0))],
        out_shape=[jax.ShapeDtypeStruct((m, d), BF16),
                   jax.ShapeDtypeStruct((1, m, LANES), F32)],
        compiler_params=_params("parallel"),
        name="prep",
    )(x, g.reshape(1, d))


def _final_norm_kernel(x_ref, g_ref, o_ref):
    x = x_ref[...]
    y = x * lax.rsqrt(jnp.mean(x * x, axis=-1, keepdims=True) + EPS)
    o_ref[...] = y * g_ref[...]


def _final_norm(x, g):
    m, d = x.shape
    bm = _tile(m, ROW_TILE)
    return pl.pallas_call(
        _final_norm_kernel,
        grid=(m // bm,),
        in_specs=[pl.BlockSpec((bm, d), lambda i: (i, 0)),
                  pl.BlockSpec((1, d), lambda i: (0, 0))],
        out_specs=pl.BlockSpec((bm, d), lambda i: (i, 0)),
        out_shape=jax.ShapeDtypeStruct((m, d), F32),
        compiler_params=_params("parallel"),
        name="final_norm",
    )(x, g.reshape(1, d))


def _mm_kernel(*refs, nk, has_ssq, has_res, has_gain, n_casts, cast_w, act, n_groups,
               post_scale, inv_k):
    refs = list(refs)
    a_ref, w_ref = refs[0], refs[1]
    pos = 2
    ssq_ref = res_ref = gain_ref = None
    if has_ssq:
        ssq_ref = refs[pos]
        pos += 1
    if has_res:
        res_ref = refs[pos]
        pos += 1
    if has_gain:
        gain_ref = refs[pos]
        pos += 1
    cast_in = refs[pos:pos + n_casts]
    pos += n_casts
    o_ref = refs[pos]
    pos += 1
    xb_ref = ssq_out_ref = None
    if has_gain:
        xb_ref, ssq_out_ref = refs[pos], refs[pos + 1]
        pos += 2
    cast_out = refs[pos:pos + n_casts]
    pos += n_casts
    acc_ref = refs[pos] if nk > 1 else None

    def cast_slabs():
        for ci, co in zip(cast_in, cast_out):
            if len(co.shape) == 2:
                co[...] = ci[...].astype(BF16)
            else:
                cb = co.shape[2]
                for c in range(co.shape[0]):
                    co[c] = ci[:, c * cb:(c + 1) * cb].astype(BF16)

    if nk == 1:
        cast_slabs()
    elif n_casts:
        pl.when(pl.program_id(2) == 0)(cast_slabs)

    def emit(new):
        xb_ref[...] = (new * gain_ref[...]).astype(BF16)
        ssq_out_ref[...] = _fold_lanes(new * new)

    def dot():
        w = w_ref[...]
        if cast_w:
            w = w.astype(BF16)
        return jnp.dot(a_ref[...], w, preferred_element_type=F32)

    if nk > 1:
        assert nk % 2 == 0 and has_res and not has_ssq and act is None and post_scale is None
        k = pl.program_id(2)

        @pl.when(k == 0)
        def _():
            acc_ref[...] = res_ref[...] + dot()

        @pl.when(jnp.logical_and(k > 0, k % 2 == 0))
        def _():
            acc_ref[...] = o_ref[...] + dot()

        @pl.when(jnp.logical_and(k % 2 == 1, k < nk - 1))
        def _():
            o_ref[...] = acc_ref[...] + dot()

        @pl.when(k == nk - 1)
        def _():
            new = acc_ref[...] + dot()
            o_ref[...] = new
            if has_gain:
                emit(new)
        return

    acc = dot()
    if has_ssq:
        acc = acc * _row_rsqrt(ssq_ref, inv_k)
    if post_scale is not None:
        acc = acc * post_scale
    if act == "relu2":
        acc = jnp.square(jnp.maximum(acc, 0.0))
    elif act == "group_softmax":
        gw = acc.shape[1] // n_groups
        parts = []
        for gi in range(n_groups):
            s = acc[:, gi * gw:(gi + 1) * gw]
            e = jnp.exp(s - jnp.max(s, axis=-1, keepdims=True))
            parts.append(e / jnp.sum(e, axis=-1, keepdims=True))
        acc = jnp.concatenate(parts, axis=1)
    if has_res:
        acc = acc + res_ref[...]
    o_ref[...] = acc.astype(o_ref.dtype)
    if has_gain:
        emit(acc)


def _mm(a, w, *, layer=None, ssq=None, post_scale=None, res=None, act=None, n_groups=1,
        gain=None, casts=(), out_dtype=BF16, bm=None, bn=None, bk=None, name="mm"):
    m, kdim = a.shape
    nl, _, n = w.shape
    if layer == "blocked":
        bn, n = n, nl * n
    rows_per_w = m if layer is not None else m // nl
    bm = _tile(rows_per_w, bm or MM_BM)
    bn = _tile(n, bn or MM_BN)
    bk = _tile(kdim, bk or MM_BK)
    nk = kdim // bk
    blocks_per_w = rows_per_w // bm
    if layer == "blocked":
        w_map = lambda i, j, k: (j, k, 0)
    elif layer is not None:
        w_map = lambda i, j, k: (layer, k, j)
    else:
        w_map = lambda i, j, k: (i // blocks_per_w, k, j)

    in_specs = [pl.BlockSpec((bm, bk), lambda i, j, k: (i, k)),
                pl.BlockSpec((None, bk, bn), w_map)]
    args = [a, w]
    if ssq is not None:
        in_specs.append(pl.BlockSpec((ssq.shape[0], bm, LANES), lambda i, j, k: (0, i, 0)))
        args.append(ssq)
    if res is not None:
        in_specs.append(pl.BlockSpec((bm, bn), lambda i, j, k: (i, j)))
        args.append(res)
    out_specs = [pl.BlockSpec((bm, bn), lambda i, j, k: (i, j))]
    out_shape = [jax.ShapeDtypeStruct((m, n), out_dtype)]
    if gain is not None:
        in_specs.append(pl.BlockSpec((1, bn), lambda i, j, k: (0, j)))
        args.append(gain.reshape(1, n))
        mode = dict(pipeline_mode=pl.Buffered(1)) if nk > 1 else {}
        out_specs += [pl.BlockSpec((bm, bn), lambda i, j, k: (i, j), **mode),
                      pl.BlockSpec((None, bm, LANES), lambda i, j, k: (j, i, 0), **mode)]
        out_shape += [jax.ShapeDtypeStruct((m, n), BF16),
                      jax.ShapeDtypeStruct((n // bn, m, LANES), F32)]
    n_main = len(out_shape)
    gi, gj = m // bm, n // bn
    for src, src_layer, *col_block in casts:
        _, kw, nw = src.shape
        cj = max(c for c in range(1, gj + 1)
                 if kw % (gi * c) == 0 and (kw // (gi * c)) % BF16_SUBLANES == 0)
        rows = kw // (gi * cj)
        slab = lambda i, j, k, cj=cj: i * cj + jnp.minimum(j, cj - 1)
        in_specs.append(pl.BlockSpec(
            (None, rows, nw), lambda i, j, k, sl=slab, l=src_layer: (l, sl(i, j, k), 0)))
        args.append(src)
        nblk = nw // col_block[0] if col_block else 1
        first = nblk if col_block else None
        out_specs.append(pl.BlockSpec(
            (first, rows, nw // nblk), lambda i, j, k, sl=slab: (0, sl(i, j, k), 0)))
        out_shape.append(jax.ShapeDtypeStruct((nblk, kw, nw // nblk), BF16))
    kern = functools.partial(
        _mm_kernel, nk=nk, has_ssq=ssq is not None, has_res=res is not None,
        has_gain=gain is not None, n_casts=len(casts), cast_w=w.dtype != BF16, act=act,
        n_groups=n_groups, post_scale=post_scale, inv_k=1.0 / kdim)
    outs = pl.pallas_call(
        kern,
        grid=(gi, gj, nk),
        in_specs=in_specs,
        out_specs=out_specs,
        out_shape=out_shape,
        scratch_shapes=[pltpu.VMEM((bm, bn), F32)] if nk > 1 else [],
        compiler_params=_params("parallel", "arbitrary", "arbitrary"),
        name=name,
    )(*args)
    main =tuple(outs[:n_main]) if gain is not None else outs[0]
    return main, list(outs[n_main:])


def _normin_kernel(*refs, n_tiles, n_chunks, n_casts, inv_d):
    x_ref, g_ref, w_ref, ssq0_ref, xb0_hbm = refs[:5]
    cast_in = refs[5:5 + n_casts]
    o_ref = refs[5 + n_casts]
    cast_out = refs[6 + n_casts:6 + 2 * n_casts]
    xb_even, xb_odd, r_even, r_odd, xb0_sem = refs[6 + 2 * n_casts:]
    i, j = pl.program_id(0), pl.program_id(1)
    ct = x_ref.shape[0]

    for ci, co in zip(cast_in, cast_out):
        cb = co.shape[2]
        for c in range(co.shape[0]):
            co[c] = ci[:, c * cb:(c + 1) * cb].astype(BF16)

    @pl.when(jnp.logical_and(i == 0, j == 0))
    def _():
        xb0_copy = pltpu.make_async_copy(xb0_hbm, xb_even, xb0_sem)
        xb0_copy.start()
        r_even[...] = _row_rsqrt(ssq0_ref, inv_d)
        xb0_copy.wait()

    rows = pl.ds(pl.multiple_of(jnp.minimum(j, n_chunks - 1) * ct, ct), ct)

    def step(xb_read, r_read, xb_write, r_write):
        x = x_ref[...]
        r_write[rows, :] = lax.rsqrt(jnp.sum(x * x, axis=-1, keepdims=True) * inv_d + EPS)
        xb_write[rows, :] = (x * g_ref[...]).astype(BF16)
        acc = jnp.dot(xb_read[...], w_ref[...], preferred_element_type=F32)
        o_ref[...] = (acc * r_read[...]).astype(o_ref.dtype)

    @pl.when(i % 2 == 0)
    def _():
        step(xb_even, r_even, xb_odd, r_odd)

    @pl.when(i % 2 == 1)
    def _():
        step(xb_odd, r_odd, xb_even, r_even)


def _normin(x, g, w, *, layer, casts=()):
    m, d = x.shape
    if layer == "blocked":
        bn, n = w.shape[2], w.shape[0] * w.shape[2]
        w_map = lambda i, j: (j, 0, 0)
    else:
        n = w.shape[2]
        bn = _tile(n, MM_BN)
        w_map = lambda i, j: (0, 0, j)
    bm = _tile(m, MM_BM)
    n_tiles, gj = m // bm, n // bn
    n_chunks = max(c for c in range(1, gj + 1) if bm % (c * BF16_SUBLANES) == 0)
    ct = bm // n_chunks
    xb0, ssq0 = _prep(x, g, rows=bm)

    chunk = lambda i, j: jnp.minimum(i + 1, n_tiles - 1) * n_chunks + jnp.minimum(j, n_chunks - 1)
    in_specs = [pl.BlockSpec((ct, d), lambda i, j: (chunk(i, j), 0)),
                pl.BlockSpec((1, d), lambda i, j: (0, 0), pipeline_mode=pl.Buffered(1)),
                pl.BlockSpec((None, d, bn), w_map),
                pl.BlockSpec((1, bm, LANES), lambda i, j: (0, 0, 0),
                             pipeline_mode=pl.Buffered(1)),
                pl.BlockSpec(memory_space=pl.ANY)]
    args = [x, g.reshape(1, d), w, ssq0, xb0]
    out_specs = [pl.BlockSpec((bm, bn), lambda i, j: (i, j))]
    out_shape = [jax.ShapeDtypeStruct((m, n), BF16)]
    for src, src_layer, cb in casts:
        _, kw, nw = src.shape
        cs = max(c for c in range(1, n_tiles * gj + 1)
                 if kw % c == 0 and (kw // c) % BF16_SUBLANES == 0)
        slab = lambda i, j, cs=cs: jnp.minimum(i * gj + j, cs - 1)
        in_specs.append(pl.BlockSpec((None, kw // cs, nw),
                                     lambda i, j, sl=slab, l=src_layer: (l, sl(i, j), 0)))
        args.append(src)
        out_specs.append(pl.BlockSpec((nw // cb, kw // cs, cb),
                                      lambda i, j, sl=slab: (0, sl(i, j), 0)))
        out_shape.append(jax.ShapeDtypeStruct((nw // cb, kw, cb), BF16))
    kern = functools.partial(_normin_kernel, n_tiles=n_tiles, n_chunks=n_chunks,
                             n_casts=len(casts), inv_d=1.0 / d)
    outs = pl.pallas_call(
        kern,
        grid=(n_tiles, gj),
        in_specs=in_specs,
        out_specs=out_specs,
        out_shape=out_shape,
        scratch_shapes=[pltpu.VMEM((bm, d), BF16), pltpu.VMEM((bm, d), BF16),
                        pltpu.VMEM((bm, 1), F32), pltpu.VMEM((bm, 1), F32),
                        pltpu.SemaphoreType.DMA(())],
        compiler_params=_params("arbitrary", "arbitrary"),
        name="normin",
    )(*args)
    return outs[0], list(outs[1:])


def _mix_rows(z_ref, halo_ref, pw_ref, ps_ref, lg_ref, lb_ref, sw_ref, sb_ref, y_ref, v_ref,
              pos0, wpool, wsgu):
    t = z_ref.shape[0]
    n_groups = len(POOL_WINDOWS)
    gc = wpool // n_groups

    halo = halo_ref[...]
    halo = jnp.where(pos0 == 0, jnp.zeros_like(halo), halo)
    row = lax.broadcasted_iota(jnp.int32, (t, t + POOL_HALO), 0)
    col = lax.broadcasted_iota(jnp.int32, (t, t + POOL_HALO), 1)
    lag = row + POOL_HALO - col
    pos = pos0 + lax.broadcasted_iota(jnp.int32, (t, 1), 0)
    for gi, win in enumerate(POOL_WINDOWS):
        cs = slice(gi * gc, (gi + 1) * gc)
        zg = z_ref[:, cs]
        band = jnp.logical_and(lag >= 0, lag < win).astype(BF16)
        wsum = jnp.dot(band, jnp.concatenate([halo[:, cs], zg], axis=0),
                       preferred_element_type=F32)
        cnt = jnp.minimum(pos + 1, win).astype(F32)
        d = wsum / cnt - zg.astype(F32)
        yg = jnp.dot(d.astype(BF16), pw_ref[gi], preferred_element_type=F32)
        y_ref[:, cs] = (yg * ps_ref[:, cs]).astype(BF16)

    gv = jax.nn.gelu(z_ref[:, wpool + wsgu:].astype(F32))
    mu = jnp.mean(gv, axis=-1, keepdims=True)
    cen = gv - mu
    var = jnp.mean(cen * cen, axis=-1, keepdims=True)
    v_ref[...] = (cen * lax.rsqrt(var + EPS) * lg_ref[...] + lb_ref[...]).astype(BF16)

    tri = (lax.broadcasted_iota(jnp.int32, (CHUNK, CHUNK), 0)
           >= lax.broadcasted_iota(jnp.int32, (CHUNK, CHUNK), 1))
    for h in range(wsgu // SGU_HEAD):
        hs = slice(h * SGU_HEAD, (h + 1) * SGU_HEAD)
        us = slice(wpool + h * SGU_HEAD, wpool + (h + 1) * SGU_HEAD)
        wm = jnp.where(tri, sw_ref[h], jnp.zeros((CHUNK, CHUNK), BF16))
        for c in range(t // CHUNK):
            rs = slice(c * CHUNK, (c + 1) * CHUNK)
            mixed = jnp.dot(wm, v_ref[rs, hs], preferred_element_type=F32) + sb_ref[:, hs]
            u = jax.nn.gelu(z_ref[rs, us].astype(F32))
            y_ref[rs, us] = (u * mixed).astype(BF16)


def _mixout_kernel(*refs, seq, wpool, wsgu, n_tiles, gj, has_cast):
    (z_ref, halo_ref, pw_ref, ps_ref, lg_ref, lb_ref, sw_ref, sb_ref,
     w_ref, res_ref, gain_ref, y0_hbm) = refs[:12]
    pos = 12
    cast_in = cast_out = None
    if has_cast:
        cast_in = refs[pos]
        pos += 1
    o_ref, xb_ref, ssq_out_ref = refs[pos:pos + 3]
    pos += 3
    if has_cast:
        cast_out = refs[pos]
        pos += 1
    y_even, y_odd, v_ref, y0_sem, w_buf, res_buf, ring_sem = refs[pos:pos + 7]
    i, j = pl.program_id(0), pl.program_id(1)
    bm = y_even.shape[0]
    bn = res_buf.shape[2]

    s = i * gj + j
    n_steps = n_tiles * gj

    def ring_copies(step_idx):
        slot = step_idx % RING
        jj = step_idx % gj
        rows = pl.ds(pl.multiple_of((step_idx // gj) * bm, bm), bm)
        cols = pl.ds(pl.multiple_of(jj * bn, bn), bn)
        return (pltpu.make_async_copy(w_ref.at[jj], w_buf.at[slot], ring_sem.at[0, slot]),
                pltpu.make_async_copy(res_ref.at[rows, cols], res_buf.at[slot],
                                      ring_sem.at[1, slot]))

    @pl.when(s == 0)
    def _():
        for first in range(RING - 1):
            for c in ring_copies(first):
                c.start()

    @pl.when(s + RING - 1 < n_steps)
    def _():
        for c in ring_copies(s + RING - 1):
            c.start()

    for c in ring_copies(s):
        c.wait()
    slot = s % RING

    if has_cast:
        cb = cast_out.shape[2]
        for c in range(cast_out.shape[0]):
            cast_out[c] = cast_in[:, c * cb:(c + 1) * cb].astype(BF16)

    @pl.when(jnp.logical_and(i == 0, j == 0))
    def _():
        y0_copy = pltpu.make_async_copy(y0_hbm, y_even, y0_sem)
        y0_copy.start()
        y0_copy.wait()

    t = z_ref.shape[0]
    row0 = jnp.minimum(i + 1, n_tiles - 1) * bm + j * t

    def step(y_read, y_write):
        chunk = y_write.at[pl.ds(pl.multiple_of(j * t, t), t), :]
        _mix_rows(z_ref, halo_ref, pw_ref, ps_ref, lg_ref, lb_ref, sw_ref, sb_ref, chunk,
                  v_ref, row0 % seq, wpool, wsgu)
        new = jnp.dot(y_read[...], w_buf[slot], preferred_element_type=F32) + res_buf[slot]
        o_ref[...] = new
        xb_ref[...] = (new * gain_ref[...]).astype(BF16)
        ssq_out_ref[...] = _fold_lanes(new * new)

    @pl.when(i % 2 == 0)
    def _():
        step(y_even, y_odd)

    @pl.when(i % 2 == 1)
    def _():
        step(y_odd, y_even)


def _mixout_bn(seq, n):
    gj = _tile(seq, MM_BM) // _tile(seq, MIXOUT_ROWS)
    assert n % (gj * LANES) == 0
    return n // gj


def _mixout(z, pool_w, pool_scale, ln_g, ln_b, sgu_w, sgu_b, w, res, gain, *, layer, seq,
            cast=None):
    m, d_in = z.shape
    _, n_groups, gc, _ = pool_w.shape
    wpool = n_groups * gc
    wsgu = (d_in - wpool) // 2
    n_heads = wsgu // SGU_HEAD
    assert n_groups == len(POOL_WINDOWS) and sgu_w.shape[1:] == (n_heads, CHUNK, CHUNK)
    gj, kdim, bn = w.shape
    n = gj * bn
    bm = _tile(seq, MM_BM)
    t = bm // gj
    n_tiles = m // bm
    assert kdim == wpool + wsgu and bn == _mixout_bn(seq, n) and n_tiles * gj >= RING - 1
    assert t % CHUNK == 0 and t % POOL_HALO == 0
    halo_blocks = t // POOL_HALO
    sb = jnp.repeat(jnp.transpose(sgu_b), SGU_HEAD, axis=1)

    y0 = _mix(z, pool_w, pool_scale, ln_g, ln_b, sgu_w, sgu_b, layer=layer, seq=seq, rows=bm)

    chunk_idx = lambda i, j: jnp.minimum(i + 1, n_tiles - 1) * gj + j
    const2 = lambda i, j: (0, 0)
    once = dict(pipeline_mode=pl.Buffered(1))
    in_specs = [pl.BlockSpec((t, d_in), lambda i, j: (chunk_idx(i, j), 0)),
                pl.BlockSpec((POOL_HALO, wpool),
                             lambda i, j: (jnp.maximum(chunk_idx(i, j) * halo_blocks - 1, 0), 0)),
                pl.BlockSpec((None, n_groups, gc, gc), lambda i, j: (layer, 0, 0, 0), **once),
                pl.BlockSpec((1, wpool), const2, **once),
                pl.BlockSpec((1, wsgu), const2, **once),
                pl.BlockSpec((1, wsgu), const2, **once),
                pl.BlockSpec((None, n_heads, CHUNK, CHUNK), lambda i, j: (layer, 0, 0, 0),
                             **once),
                pl.BlockSpec((CHUNK, wsgu), const2, **once),
                pl.BlockSpec(memory_space=pl.ANY),
                pl.BlockSpec(memory_space=pl.ANY),
                pl.BlockSpec((1, bn), lambda i, j: (0, j)),
                pl.BlockSpec(memory_space=pl.ANY)]
    args = [z, z, pool_w, pool_scale.reshape(1, wpool), ln_g.reshape(1, wsgu),
            ln_b.reshape(1, wsgu), sgu_w, sb, w, res, gain.reshape(1, n), y0]
    out_specs = [pl.BlockSpec((bm, bn), lambda i, j: (i, j)),
                 pl.BlockSpec((bm, bn), lambda i, j: (i, j)),
                 pl.BlockSpec((None, bm, LANES), lambda i, j: (j, i, 0))]
    out_shape = [jax.ShapeDtypeStruct((m, n), F32),
                 jax.ShapeDtypeStruct((m, n), BF16),
                 jax.ShapeDtypeStruct((gj, m, LANES), F32)]
    if cast is not None:
        src, src_layer, cb = cast
        _, kw, nw = src.shape
        steps = n_tiles * gj
        cs = max(c for c in range(1, steps + 1)
                 if kw % c == 0 and (kw // c) % BF16_SUBLANES == 0)
        slab = lambda i, j: jnp.minimum(i * gj + j, cs - 1)
        in_specs.append(pl.BlockSpec((None, kw // cs, nw),
                                     lambda i, j: (src_layer, slab(i, j), 0)))
        args.append(src)
        out_specs.append(pl.BlockSpec((nw // cb, kw // cs, cb),
                                      lambda i, j: (0, slab(i, j), 0)))
        out_shape.append(jax.ShapeDtypeStruct((nw // cb, kw, cb), BF16))
    kern = functools.partial(_mixout_kernel, seq=seq, wpool=wpool, wsgu=wsgu, n_tiles=n_tiles,
                             gj=gj, has_cast=cast is not None)
    outs = pl.pallas_call(
        kern,
        grid=(n_tiles, gj),
        in_specs=in_specs,
        out_specs=out_specs,
        out_shape=out_shape,
        scratch_shapes=[pltpu.VMEM((bm, kdim), BF16), pltpu.VMEM((bm, kdim), BF16),
                        pltpu.VMEM((t, wsgu), BF16), pltpu.SemaphoreType.DMA(()),
                        pltpu.VMEM((RING, kdim, bn), BF16), pltpu.VMEM((RING, bm, bn), F32),
                        pltpu.SemaphoreType.DMA((2, RING))],
        compiler_params=_params("arbitrary", "arbitrary"),
        name="mixout",
    )(*args)
    return outs[0], outs[1], outs[2], (outs[3] if cast is not None else None)


def _mix_kernel(z_ref, halo_ref, pw_ref, ps_ref, lg_ref, lb_ref, sw_ref, sb_ref,
                y_ref, v_ref, *, seq, wpool, wsgu):
    pos0 = (pl.program_id(0) * z_ref.shape[0]) % seq
    _mix_rows(z_ref, halo_ref, pw_ref, ps_ref, lg_ref, lb_ref, sw_ref, sb_ref, y_ref, v_ref,
              pos0, wpool, wsgu)


def _mix(z, pool_w, pool_scale, ln_g, ln_b, sgu_w, sgu_b, *, layer, seq, rows):
    m, d_in = rows, z.shape[1]
    _, n_groups, gc, _ = pool_w.shape
    wpool = n_groups * gc
    wsgu = (d_in - wpool) // 2
    n_heads = wsgu // SGU_HEAD
    assert n_groups == len(POOL_WINDOWS) and sgu_w.shape[1:] == (n_heads, CHUNK, CHUNK)
    t = _tile(seq, ROW_TILE)
    assert t % CHUNK == 0 and t % POOL_HALO == 0
    halo_blocks = t // POOL_HALO
    sb = jnp.repeat(jnp.transpose(sgu_b), SGU_HEAD, axis=1)
    kern = functools.partial(_mix_kernel, seq=seq, wpool=wpool, wsgu=wsgu)
    const2 = lambda i: (0, 0)
    return pl.pallas_call(
        kern,
        grid=(m // t,),
        in_specs=[pl.BlockSpec((t, d_in), lambda i: (i, 0)),
                  pl.BlockSpec((POOL_HALO, wpool),
                               lambda i: (jnp.maximum(i * halo_blocks - 1, 0), 0)),
                  pl.BlockSpec((None, n_groups, gc, gc), lambda i: (layer, 0, 0, 0)),
                  pl.BlockSpec((1, wpool), const2),
                  pl.BlockSpec((1, wsgu), const2),
                  pl.BlockSpec((1, wsgu), const2),
                  pl.BlockSpec((None, n_heads, CHUNK, CHUNK), lambda i: (layer, 0, 0, 0)),
                  pl.BlockSpec((CHUNK, wsgu), const2)],
        out_specs=pl.BlockSpec((t, wpool + wsgu), lambda i: (i, 0)),
        out_shape=jax.ShapeDtypeStruct((m, wpool + wsgu), BF16),
        scratch_shapes=[pltpu.VMEM((t, wsgu), BF16)],
        compiler_params=_params("parallel"),
        name="mix",
    )(z, z, pool_w, pool_scale.reshape(1, wpool), ln_g.reshape(1, wsgu),
      ln_b.reshape(1, wsgu), sgu_w, sb)


def _fold_qk_kernel(wq_ref, k_ref, a_ref):
    n_mem = a_ref.shape[2]
    s = lax.dot_general(wq_ref[...].astype(BF16), k_ref[...], (((1,), (1,)), ((), ())),
                        preferred_element_type=F32)
    for b in range(a_ref.shape[0]):
        a_ref[b] = s[:, b * n_mem:(b + 1) * n_mem].astype(a_ref.dtype)


def _fold_vo_kernel(v_ref, wo_ref, b_ref):
    n_mem = b_ref.shape[1]
    s = jnp.dot(v_ref[...], wo_ref[...].astype(BF16), preferred_element_type=F32)
    for b in range(b_ref.shape[0]):
        b_ref[b] = s[b * n_mem:(b + 1) * n_mem].astype(b_ref.dtype)


def _fold_qk(wq, k, *, layer, n_batch, n_mem):
    d = wq.shape[1]
    dh = d // N_XHEADS
    rb = _tile(d, FOLD_TILE)
    return pl.pallas_call(
        _fold_qk_kernel,
        grid=(N_XHEADS, d // rb),
        in_specs=[pl.BlockSpec((None, rb, dh), lambda h, r: (layer, r, h)),
                  pl.BlockSpec((n_batch * n_mem, dh), lambda h, r: (0, h))],
        out_specs=pl.BlockSpec((n_batch, rb, n_mem), lambda h, r: (0, r, h)),
        out_shape=jax.ShapeDtypeStruct((n_batch, d, N_XHEADS * n_mem), BF16),
        compiler_params=_params("parallel", "parallel"),
        name="fold_qk",
    )(wq, k)


def _fold_vo(v, wo, *, layer, n_batch, n_mem):
    d = wo.shape[2]
    dh = d // N_XHEADS
    cb = _tile(d, FOLD_TILE)
    return pl.pallas_call(
        _fold_vo_kernel,
        grid=(N_XHEADS, d // cb),
        in_specs=[pl.BlockSpec((n_batch * n_mem, dh), lambda h, c: (0, h)),
                  pl.BlockSpec((None, dh, cb), lambda h, c: (layer, h, c))],
        out_specs=pl.BlockSpec((n_batch, n_mem, cb), lambda h, c: (0, h, c)),
        out_shape=jax.ShapeDtypeStruct((n_batch, N_XHEADS * n_mem, d), BF16),
        compiler_params=_params("parallel", "parallel"),
        name="fold_vo",
    )(v, wo)


def kernel(x, mem, ln_mix, w_in, pool_w, pool_scale, sgu_ln_g, sgu_ln_b, sgu_w, sgu_b,
           w_out, ln_x, ln_mem, w_q, w_k, w_v, w_o, ln_ffn, w_up, w_down, ln_final):
    n_batch, seq, d = x.shape
    n_mem = mem.shape[1]
    depth = w_in.shape[0]
    dh = d // N_XHEADS
    x = x.reshape(n_batch * seq, d)
    memf = mem.reshape(n_batch * n_mem, d)

    wb_in = w_in[:1].astype(BF16)
    pool_w, sgu_w = pool_w.astype(BF16), sgu_w.astype(BF16)

    blocked = lambda w, l: (w, l, min(MM_BN, w.shape[2]))

    for l in range(depth):
        z, (wb_out,) = _normin(x, ln_mix[l], wb_in, layer=0 if l == 0 else "blocked",
                               casts=[(w_out, l, _mixout_bn(seq, d))])
        x, xb, ssq, cast = _mixout(z, pool_w, pool_scale[l], sgu_ln_g[l], sgu_ln_b[l], sgu_w,
                                   sgu_b[l], wb_out, x, ln_x[l], layer=l, seq=seq,
                                   cast=blocked(w_up, 0) if l == 0 else None)
        if l == 0:
            wb_up = cast

        mb, mssq = _prep(memf, ln_mem[l])
        k, _ = _mm(mb, w_k, layer=l, ssq=mssq, bn=MM_BN_RES, name="mm_k")
        v, _ = _mm(mb, w_v, layer=l, ssq=mssq, bn=MM_BN_RES, name="mm_v")
        a_fold = _fold_qk(w_q, k, layer=l, n_batch=n_batch, n_mem=n_mem)
        b_fold = _fold_vo(v, w_o, layer=l, n_batch=n_batch, n_mem=n_mem)
        p, _ = _mm(xb, a_fold, ssq=ssq, post_scale=dh ** -0.5, act="group_softmax",
                   n_groups=N_XHEADS, bn=N_XHEADS * n_mem, name="mm_scores")
        (x, xb, ssq), _ = _mm(p, b_fold, res=x, gain=ln_ffn[l], out_dtype=F32,
                              bm=ATTN_OUT_BM, bn=d, name="mm_attn_out")

        last = l + 1 == depth
        nxt = [] if last else [blocked(w_up, l + 1), blocked(w_in, l + 1)]
        hid, cast = _mm(xb, wb_up, layer="blocked", ssq=ssq, act="relu2",
                        casts=[blocked(w_down, l)] + nxt, name="mm_up")
        wb_down = cast[0]
        if not last:
            wb_up, wb_in = cast[1:]
        x, _ = _mm(hid, wb_down, layer="blocked", res=x, out_dtype=F32, bk=MM_BK_DOWN,
                   name="mm_down")

    return _final_norm(x, ln_final).reshape(n_batch, seq, d)
```

```python
import functools

import jax
import jax.numpy as jnp
from jax import lax
from jax.experimental import pallas as pl
from jax.experimental.pallas import tpu as pltpu

EPS = 1e-6
POOL_WINDOWS = (2, 4, 8, 16)
POOL_HALO = 16
CHUNK = 128
SGU_HEAD = 128
N_XHEADS = 4
LANES = 128
BF16_SUBLANES = 16

V7X_VMEM_LIMIT_BYTES = 63 * 1024 * 1024

MM_BM = 1024
MM_BN = 1024
MM_BN_RES = 512
MM_BK = 4096
MM_BK_DOWN = 4096
ATTN_OUT_BM = 512
MIXOUT_ROWS = 128
ROW_TILE = 256
FOLD_TILE = 1024

F32 = jnp.float32
BF16 = jnp.bfloat16


def _tile(dim, pref):
    t = min(dim, pref)
    assert dim % t == 0, (dim, pref)
    return t


def _params(*sem):
    return pltpu.CompilerParams(dimension_semantics=sem,
                                vmem_limit_bytes=V7X_VMEM_LIMIT_BYTES)


def _fold_lanes(v):
    acc = v[:, :LANES]
    for c in range(1, v.shape[1] // LANES):
        acc = acc + v[:, c * LANES:(c + 1) * LANES]
    return acc


def _row_rsqrt(ssq_ref, inv_width):
    s = ssq_ref[0]
    for c in range(1, ssq_ref.shape[0]):
        s = s + ssq_ref[c]
    return lax.rsqrt(jnp.sum(s, axis=-1, keepdims=True) * inv_width + EPS)


def _prep_kernel(x_ref, g_ref, xb_ref, ssq_ref):
    x = x_ref[...]
    ssq_ref[...] = _fold_lanes(x * x)
    xb_ref[...] = (x * g_ref[...]).astype(BF16)


def _prep(x, g, rows=None):
    m, d = rows or x.shape[0], x.shape[1]
    bm = _tile(m, ROW_TILE)
    return pl.pallas_call(
        _prep_kernel,
        grid=(m // bm,),
        in_specs=[pl.BlockSpec((bm, d), lambda i: (i, 0)),
                  pl.BlockSpec((1, d), lambda i: (0, 0))],
        out_specs=[pl.BlockSpec((bm, d), lambda i: (i, 0)),
                   pl.BlockSpec((None, bm, LANES), lambda i: (0, i, 0))],
        out_shape=[jax.ShapeDtypeStruct((m, d), BF16),
                   jax.ShapeDtypeStruct((1, m, LANES), F32)],
        compiler_params=_params("parallel"),
        name="prep",
    )(x, g.reshape(1, d))


def _final_norm_kernel(x_ref, g_ref, o_ref):
    x = x_ref[...]
    y = x * lax.rsqrt(jnp.mean(x * x, axis=-1, keepdims=True) + EPS)
    o_ref[...] = y * g_ref[...]


def _final_norm(x, g):
    m, d = x.shape
    bm = _tile(m, ROW_TILE)
    return pl.pallas_call(
        _final_norm_kernel,
        grid=(m // bm,),
        in_specs=[pl.BlockSpec((bm, d), lambda i: (i, 0)),
                  pl.BlockSpec((1, d), lambda i: (0, 0))],
        out_specs=pl.BlockSpec((bm, d), lambda i: (i, 0)),
        out_shape=jax.ShapeDtypeStruct((m, d), F32),
        compiler_params=_params("parallel"),
        name="final_norm",
    )(x, g.reshape(1, d))


def _mm_kernel(*refs, nk, has_ssq, has_res, has_gain, n_casts, cast_w, act, n_groups,
               post_scale, inv_k):
    refs = list(refs)
    a_ref, w_ref = refs[0], refs[1]
    pos = 2
    ssq_ref = res_ref = gain_ref = None
    if has_ssq:
        ssq_ref = refs[pos]
        pos += 1
    if has_res:
        res_ref = refs[pos]
        pos += 1
    if has_gain:
        gain_ref = refs[pos]
        pos += 1
    cast_in = refs[pos:pos + n_casts]
    pos += n_casts
    o_ref = refs[pos]
    pos += 1
    xb_ref = ssq_out_ref = None
    if has_gain:
        xb_ref, ssq_out_ref = refs[pos], refs[pos + 1]
        pos += 2
    cast_out = refs[pos:pos + n_casts]
    pos += n_casts
    acc_ref = refs[pos] if nk > 1 else None

    def cast_slabs():
        for ci, co in zip(cast_in, cast_out):
            if len(co.shape) == 2:
                co[...] = ci[...].astype(BF16)
            else:
                cb = co.shape[2]
                for c in range(co.shape[0]):
                    co[c] = ci[:, c * cb:(c + 1) * cb].astype(BF16)

    if nk == 1:
        cast_slabs()
    elif n_casts:
        pl.when(pl.program_id(2) == 0)(cast_slabs)

    def emit(new):
        xb_ref[...] = (new * gain_ref[...]).astype(BF16)
        ssq_out_ref[...] = _fold_lanes(new * new)

    def dot():
        w = w_ref[...]
        if cast_w:
            w = w.astype(BF16)
        return jnp.dot(a_ref[...], w, preferred_element_type=F32)

    if nk > 1:
        assert nk % 2 == 0 and has_res and not has_ssq and act is None and post_scale is None
        k = pl.program_id(2)

        @pl.when(k == 0)
        def _():
            acc_ref[...] = res_ref[...] + dot()

        @pl.when(jnp.logical_and(k > 0, k % 2 == 0))
        def _():
            acc_ref[...] = o_ref[...] + dot()

        @pl.when(jnp.logical_and(k % 2 == 1, k < nk - 1))
        def _():
            o_ref[...] = acc_ref[...] + dot()

        @pl.when(k == nk - 1)
        def _():
            new = acc_ref[...] + dot()
            o_ref[...] = new
            if has_gain:
                emit(new)
        return

    acc = dot()
    if has_ssq:
        acc = acc * _row_rsqrt(ssq_ref, inv_k)
    if post_scale is not None:
        acc = acc * post_scale
    if act == "relu2":
        acc = jnp.square(jnp.maximum(acc, 0.0))
    elif act == "group_softmax":
        gw = acc.shape[1] // n_groups
        parts = []
        for gi in range(n_groups):
            s = acc[:, gi * gw:(gi + 1) * gw]
            e = jnp.exp(s - jnp.max(s, axis=-1, keepdims=True))
            parts.append(e / jnp.sum(e, axis=-1, keepdims=True))
        acc = jnp.concatenate(parts, axis=1)
    if has_res:
        acc = acc + res_ref[...]
    o_ref[...] = acc.astype(o_ref.dtype)
    if has_gain:
        emit(acc)


def _mm(a, w, *, layer=None, ssq=None, post_scale=None, res=None, act=None, n_groups=1,
        gain=None, casts=(), out_dtype=BF16, bm=None, bn=None, bk=None, name="mm"):
    m, kdim = a.shape
    nl, _, n = w.shape
    if layer == "blocked":
        bn, n = n, nl * n
    rows_per_w = m if layer is not None else m // nl
    bm = _tile(rows_per_w, bm or MM_BM)
    bn = _tile(n, bn or MM_BN)
    bk = _tile(kdim, bk or MM_BK)
    nk = kdim // bk
    blocks_per_w = rows_per_w // bm
    if layer == "blocked":
        w_map = lambda i, j, k: (j, k, 0)
    elif layer is not None:
        w_map = lambda i, j, k: (layer, k, j)
    else:
        w_map = lambda i, j, k: (i // blocks_per_w, k, j)

    in_specs = [pl.BlockSpec((bm, bk), lambda i, j, k: (i, k)),
                pl.BlockSpec((None, bk, bn), w_map)]
    args = [a, w]
    if ssq is not None:
        in_specs.append(pl.BlockSpec((ssq.shape[0], bm, LANES), lambda i, j, k: (0, i, 0)))
        args.append(ssq)
    if res is not None:
        in_specs.append(pl.BlockSpec((bm, bn), lambda i, j, k: (i, j)))
        args.append(res)
    out_specs = [pl.BlockSpec((bm, bn), lambda i, j, k: (i, j))]
    out_shape = [jax.ShapeDtypeStruct((m, n), out_dtype)]
    if gain is not None:
        in_specs.append(pl.BlockSpec((1, bn), lambda i, j, k: (0, j)))
        args.append(gain.reshape(1, n))
        mode = dict(pipeline_mode=pl.Buffered(1)) if nk > 1 else {}
        out_specs += [pl.BlockSpec((bm, bn), lambda i, j, k: (i, j), **mode),
                      pl.BlockSpec((None, bm, LANES), lambda i, j, k: (j, i, 0), **mode)]
        out_shape += [jax.ShapeDtypeStruct((m, n), BF16),
                      jax.ShapeDtypeStruct((n // bn, m, LANES), F32)]
    n_main = len(out_shape)
    gi, gj = m // bm, n // bn
    for src, src_layer, *col_block in casts:
        _, kw, nw = src.shape
        cj = max(c for c in range(1, gj + 1)
                 if kw % (gi * c) == 0 and (kw // (gi * c)) % BF16_SUBLANES == 0)
        rows = kw // (gi * cj)
        slab = lambda i, j, k, cj=cj: i * cj + jnp.minimum(j, cj - 1)
        in_specs.append(pl.BlockSpec(
            (None, rows, nw), lambda i, j, k, sl=slab, l=src_layer: (l, sl(i, j, k), 0)))
        args.append(src)
        nblk = nw // col_block[0] if col_block else 1
        first = nblk if col_block else None
        out_specs.append(pl.BlockSpec(
            (first, rows, nw // nblk), lambda i, j, k, sl=slab: (0, sl(i, j, k), 0)))
        out_shape.append(jax.ShapeDtypeStruct((nblk, kw, nw // nblk), BF16))
    kern = functools.partial(
        _mm_kernel, nk=nk, has_ssq=ssq is not None, has_res=res is not None,
        has_gain=gain is not None, n_casts=len(casts), cast_w=w.dtype != BF16, act=act,
        n_groups=n_groups, post_scale=post_scale, inv_k=1.0 / kdim)
    outs = pl.pallas_call(
        kern,
        grid=(gi, gj, nk),
        in_specs=in_specs,
        out_specs=out_specs,
        out_shape=out_shape,
        scratch_shapes=[pltpu.VMEM((bm, bn), F32)] if nk > 1 else [],
        compiler_params=_params("parallel", "arbitrary", "arbitrary"),
        name=name,
    )(*args)
    main =tuple(outs[:n_main]) if gain is not None else outs[0]
    return main, list(outs[n_main:])


def _normin_kernel(*refs, n_tiles, n_chunks, n_casts, inv_d):
    x_ref, g_ref, w_ref, ssq0_ref, xb0_hbm = refs[:5]
    cast_in = refs[5:5 + n_casts]
    o_ref = refs[5 + n_casts]
    cast_out = refs[6 + n_casts:6 + 2 * n_casts]
    xb_even, xb_odd, r_even, r_odd, xb0_sem = refs[6 + 2 * n_casts:]
    i, j = pl.program_id(0), pl.program_id(1)
    ct = x_ref.shape[0]

    for ci, co in zip(cast_in, cast_out):
        cb = co.shape[2]
        for c in range(co.shape[0]):
            co[c] = ci[:, c * cb:(c + 1) * cb].astype(BF16)

    @pl.when(jnp.logical_and(i == 0, j == 0))
    def _():
        xb0_copy = pltpu.make_async_copy(xb0_hbm, xb_even, xb0_sem)
        xb0_copy.start()
        r_even[...] = _row_rsqrt(ssq0_ref, inv_d)
        xb0_copy.wait()

    rows = pl.ds(pl.multiple_of(jnp.minimum(j, n_chunks - 1) * ct, ct), ct)

    def step(xb_read, r_read, xb_write, r_write):
        x = x_ref[...]
        r_write[rows, :] = lax.rsqrt(jnp.sum(x * x, axis=-1, keepdims=True) * inv_d + EPS)
        xb_write[rows, :] = (x * g_ref[...]).astype(BF16)
        acc = jnp.dot(xb_read[...], w_ref[...], preferred_element_type=F32)
        o_ref[...] = (acc * r_read[...]).astype(o_ref.dtype)

    @pl.when(i % 2 == 0)
    def _():
        step(xb_even, r_even, xb_odd, r_odd)

    @pl.when(i % 2 == 1)
    def _():
        step(xb_odd, r_odd, xb_even, r_even)


def _normin(x, g, w, *, layer, casts=()):
    m, d = x.shape
    if layer == "blocked":
        bn, n = w.shape[2], w.shape[0] * w.shape[2]
        w_map = lambda i, j: (j, 0, 0)
    else:
        n = w.shape[2]
        bn = _tile(n, MM_BN)
        w_map = lambda i, j: (0, 0, j)
    bm = _tile(m, MM_BM)
    n_tiles, gj = m // bm, n // bn
    n_chunks = max(c for c in range(1, gj + 1) if bm % (c * BF16_SUBLANES) == 0)
    ct = bm // n_chunks
    xb0, ssq0 = _prep(x, g, rows=bm)

    chunk = lambda i, j: jnp.minimum(i + 1, n_tiles - 1) * n_chunks + jnp.minimum(j, n_chunks - 1)
    in_specs = [pl.BlockSpec((ct, d), lambda i, j: (chunk(i, j), 0)),
                pl.BlockSpec((1, d), lambda i, j: (0, 0), pipeline_mode=pl.Buffered(1)),
                pl.BlockSpec((None, d, bn), w_map),
                pl.BlockSpec((1, bm, LANES), lambda i, j: (0, 0, 0),
                             pipeline_mode=pl.Buffered(1)),
                pl.BlockSpec(memory_space=pl.ANY)]
    args = [x, g.reshape(1, d), w, ssq0, xb0]
    out_specs = [pl.BlockSpec((bm, bn), lambda i, j: (i, j))]
    out_shape = [jax.ShapeDtypeStruct((m, n), BF16)]
    for src, src_layer, cb in casts:
        _, kw, nw = src.shape
        cs = max(c for c in range(1, n_tiles * gj + 1)
                 if kw % c == 0 and (kw // c) % BF16_SUBLANES == 0)
        slab = lambda i, j, cs=cs: jnp.minimum(i * gj + j, cs - 1)
        in_specs.append(pl.BlockSpec((None, kw // cs, nw),
                                     lambda i, j, sl=slab, l=src_layer: (l, sl(i, j), 0)))
        args.append(src)
        out_specs.append(pl.BlockSpec((nw // cb, kw // cs, cb),
                                      lambda i, j, sl=slab: (0, sl(i, j), 0)))
        out_shape.append(jax.ShapeDtypeStruct((nw // cb, kw, cb), BF16))
    kern = functools.partial(_normin_kernel, n_tiles=n_tiles, n_chunks=n_chunks,
                             n_casts=len(casts), inv_d=1.0 / d)
    outs = pl.pallas_call(
        kern,
        grid=(n_tiles, gj),
        in_specs=in_specs,
        out_specs=out_specs,
        out_shape=out_shape,
        scratch_shapes=[pltpu.VMEM((bm, d), BF16), pltpu.VMEM((bm, d), BF16),
                        pltpu.VMEM((bm, 1), F32), pltpu.VMEM((bm, 1), F32),
                        pltpu.SemaphoreType.DMA(())],
        compiler_params=_params("arbitrary", "arbitrary"),
        name="normin",
    )(*args)
    return outs[0], list(outs[1:])


def _mm_kv_kernel(a_ref, ssq_ref, wk_ref, wv_ref, k_ref, v_ref, *, inv_k):
    a = a_ref[...]
    r = _row_rsqrt(ssq_ref, inv_k)
    for w_ref, o_ref in ((wk_ref, k_ref), (wv_ref, v_ref)):
        acc = jnp.dot(a, w_ref[...].astype(BF16), preferred_element_type=F32)
        o_ref[...] = (acc * r).astype(o_ref.dtype)


def _mm_kv(a, ssq, w_k, w_v, *, layer):
    m, kdim = a.shape
    n = w_k.shape[2]
    bn = _tile(n, MM_BN_RES)
    w_spec = pl.BlockSpec((None, kdim, bn), lambda j: (layer, 0, j))
    o_spec = pl.BlockSpec((m, bn), lambda j: (0, j))
    return pl.pallas_call(
        functools.partial(_mm_kv_kernel, inv_k=1.0 / kdim),
        grid=(n // bn,),
        in_specs=[pl.BlockSpec((m, kdim), lambda j: (0, 0)),
                  pl.BlockSpec((ssq.shape[0], m, LANES), lambda j: (0, 0, 0)),
                  w_spec, w_spec],
        out_specs=[o_spec, o_spec],
        out_shape=[jax.ShapeDtypeStruct((m, n), BF16)] * 2,
        compiler_params=_params("parallel"),
        name="mm_kv",
    )(a, ssq, w_k, w_v)


def _mix_rows(z_ref, halo_ref, pw_ref, ps_ref, lg_ref, lb_ref, sw_ref, sb_ref, y_ref, v_ref,
              pos0, wpool, wsgu):
    t = z_ref.shape[0]
    n_groups = len(POOL_WINDOWS)
    gc = wpool // n_groups

    halo = halo_ref[...]
    halo = jnp.where(pos0 == 0, jnp.zeros_like(halo), halo)
    row = lax.broadcasted_iota(jnp.int32, (t, t + POOL_HALO), 0)
    col = lax.broadcasted_iota(jnp.int32, (t, t + POOL_HALO), 1)
    lag = row + POOL_HALO - col
    pos = pos0 + lax.broadcasted_iota(jnp.int32, (t, 1), 0)
    for gi, win in enumerate(POOL_WINDOWS):
        cs = slice(gi * gc, (gi + 1) * gc)
        zg = z_ref[:, cs]
        band = jnp.logical_and(lag >= 0, lag < win).astype(BF16)
        wsum = jnp.dot(band, jnp.concatenate([halo[:, cs], zg], axis=0),
                       preferred_element_type=F32)
        cnt = jnp.minimum(pos + 1, win).astype(F32)
        d = wsum / cnt - zg.astype(F32)
        yg = jnp.dot(d.astype(BF16), pw_ref[gi], preferred_element_type=F32)
        y_ref[:, cs] = (yg * ps_ref[:, cs]).astype(BF16)

    gv = jax.nn.gelu(z_ref[:, wpool + wsgu:].astype(F32))
    mu = jnp.mean(gv, axis=-1, keepdims=True)
    cen = gv - mu
    var = jnp.mean(cen * cen, axis=-1, keepdims=True)
    v_ref[...] = (cen * lax.rsqrt(var + EPS) * lg_ref[...] + lb_ref[...]).astype(BF16)

    tri = (lax.broadcasted_iota(jnp.int32, (CHUNK, CHUNK), 0)
           >= lax.broadcasted_iota(jnp.int32, (CHUNK, CHUNK), 1))
    for h in range(wsgu // SGU_HEAD):
        hs = slice(h * SGU_HEAD, (h + 1) * SGU_HEAD)
        us = slice(wpool + h * SGU_HEAD, wpool + (h + 1) * SGU_HEAD)
        wm = jnp.where(tri, sw_ref[h], jnp.zeros((CHUNK, CHUNK), BF16))
        for c in range(t // CHUNK):
            rs = slice(c * CHUNK, (c + 1) * CHUNK)
            mixed = jnp.dot(wm, v_ref[rs, hs], preferred_element_type=F32) + sb_ref[:, hs]
            u = jax.nn.gelu(z_ref[rs, us].astype(F32))
            y_ref[rs, us] = (u * mixed).astype(BF16)


def _mixout_kernel(*refs, seq, wpool, wsgu, n_tiles, has_cast):
    (z_ref, halo_ref, pw_ref, ps_ref, lg_ref, lb_ref, sw_ref, sb_ref,
     w_ref, res_ref, gain_ref, y0_hbm) = refs[:12]
    pos = 12
    cast_in = cast_out = None
    if has_cast:
        cast_in = refs[pos]
        pos += 1
    o_ref, xb_ref, ssq_out_ref = refs[pos:pos + 3]
    pos += 3
    if has_cast:
        cast_out = refs[pos]
        pos += 1
    y_even, y_odd, v_ref, y0_sem = refs[pos:pos + 4]
    i, j = pl.program_id(0), pl.program_id(1)
    bm = y_even.shape[0]

    if has_cast:
        cb = cast_out.shape[2]
        for c in range(cast_out.shape[0]):
            cast_out[c] = cast_in[:, c * cb:(c + 1) * cb].astype(BF16)

    @pl.when(jnp.logical_and(i == 0, j == 0))
    def _():
        y0_copy = pltpu.make_async_copy(y0_hbm, y_even, y0_sem)
        y0_copy.start()
        y0_copy.wait()

    t = z_ref.shape[0]
    row0 = jnp.minimum(i + 1, n_tiles - 1) * bm + j * t

    def step(y_read, y_write):
        chunk = y_write.at[pl.ds(pl.multiple_of(j * t, t), t), :]
        _mix_rows(z_ref, halo_ref, pw_ref, ps_ref, lg_ref, lb_ref, sw_ref, sb_ref, chunk,
                  v_ref, row0 % seq, wpool, wsgu)
        new = jnp.dot(y_read[...], w_ref[...], preferred_element_type=F32) + res_ref[...]
        o_ref[...] = new
        xb_ref[...] = (new * gain_ref[...]).astype(BF16)
        ssq_out_ref[...] = _fold_lanes(new * new)

    @pl.when(i % 2 == 0)
    def _():
        step(y_even, y_odd)

    @pl.when(i % 2 == 1)
    def _():
        step(y_odd, y_even)


def _mixout_bn(seq, n):
    gj = _tile(seq, MM_BM) // _tile(seq, MIXOUT_ROWS)
    assert n % (gj * LANES) == 0
    return n // gj


def _mixout(z, pool_w, pool_scale, ln_g, ln_b, sgu_w, sgu_b, w, res, gain, *, layer, seq,
            cast=None):
    m, d_in = z.shape
    _, n_groups, gc, _ = pool_w.shape
    wpool = n_groups * gc
    wsgu = (d_in - wpool) // 2
    n_heads = wsgu // SGU_HEAD
    assert n_groups == len(POOL_WINDOWS) and sgu_w.shape[1:] == (n_heads, CHUNK, CHUNK)
    gj, kdim, bn = w.shape
    n = gj * bn
    bm = _tile(seq, MM_BM)
    t = bm // gj
    n_tiles = m // bm
    assert kdim == wpool + wsgu and bn == _mixout_bn(seq, n)
    assert t % CHUNK == 0 and t % POOL_HALO == 0
    halo_blocks = t // POOL_HALO
    sb = jnp.repeat(jnp.transpose(sgu_b), SGU_HEAD, axis=1)

    y0 = _mix(z, pool_w, pool_scale, ln_g, ln_b, sgu_w, sgu_b, layer=layer, seq=seq, rows=bm)

    chunk_idx = lambda i, j: jnp.minimum(i + 1, n_tiles - 1) * gj + j
    const2 = lambda i, j: (0, 0)
    once = dict(pipeline_mode=pl.Buffered(1))
    in_specs = [pl.BlockSpec((t, d_in), lambda i, j: (chunk_idx(i, j), 0)),
                pl.BlockSpec((POOL_HALO, wpool),
                             lambda i, j: (jnp.maximum(chunk_idx(i, j) * halo_blocks - 1, 0), 0)),
                pl.BlockSpec((None, n_groups, gc, gc), lambda i, j: (layer, 0, 0, 0), **once),
                pl.BlockSpec((1, wpool), const2, **once),
                pl.BlockSpec((1, wsgu), const2, **once),
                pl.BlockSpec((1, wsgu), const2, **once),
                pl.BlockSpec((None, n_heads, CHUNK, CHUNK), lambda i, j: (layer, 0, 0, 0),
                             **once),
                pl.BlockSpec((CHUNK, wsgu), const2, **once),
                pl.BlockSpec((None, kdim, bn), lambda i, j: (j, 0, 0)),
                pl.BlockSpec((bm, bn), lambda i, j: (i, j)),
                pl.BlockSpec((1, bn), lambda i, j: (0, j)),
                pl.BlockSpec(memory_space=pl.ANY)]
    args = [z, z, pool_w, pool_scale.reshape(1, wpool), ln_g.reshape(1, wsgu),
            ln_b.reshape(1, wsgu), sgu_w, sb, w, res, gain.reshape(1, n), y0]
    out_specs = [pl.BlockSpec((bm, bn), lambda i, j: (i, j)),
                 pl.BlockSpec((bm, bn), lambda i, j: (i, j)),
                 pl.BlockSpec((None, bm, LANES), lambda i, j: (j, i, 0))]
    out_shape = [jax.ShapeDtypeStruct((m, n), F32),
                 jax.ShapeDtypeStruct((m, n), BF16),
                 jax.ShapeDtypeStruct((gj, m, LANES), F32)]
    if cast is not None:
        src, src_layer, cb = cast
        _, kw, nw = src.shape
        steps = n_tiles * gj
        cs = max(c for c in range(1, steps + 1)
                 if kw % c == 0 and (kw // c) % BF16_SUBLANES == 0)
        slab = lambda i, j: jnp.minimum(i * gj + j, cs - 1)
        in_specs.append(pl.BlockSpec((None, kw // cs, nw),
                                     lambda i, j: (src_layer, slab(i, j), 0)))
        args.append(src)
        out_specs.append(pl.BlockSpec((nw // cb, kw // cs, cb),
                                      lambda i, j: (0, slab(i, j), 0)))
        out_shape.append(jax.ShapeDtypeStruct((nw // cb, kw, cb), BF16))
    kern = functools.partial(_mixout_kernel, seq=seq, wpool=wpool, wsgu=wsgu, n_tiles=n_tiles,
                             has_cast=cast is not None)
    outs = pl.pallas_call(
        kern,
        grid=(n_tiles, gj),
        in_specs=in_specs,
        out_specs=out_specs,
        out_shape=out_shape,
        scratch_shapes=[pltpu.VMEM((bm, kdim), BF16), pltpu.VMEM((bm, kdim), BF16),
                        pltpu.VMEM((t, wsgu), BF16), pltpu.SemaphoreType.DMA(())],
        compiler_params=_params("arbitrary", "arbitrary"),
        name="mixout",
    )(*args)
    return outs[0], outs[1], outs[2], (outs[3] if cast is not None else None)


def _mix_kernel(z_ref, halo_ref, pw_ref, ps_ref, lg_ref, lb_ref, sw_ref, sb_ref,
                y_ref, v_ref, *, seq, wpool, wsgu):
    pos0 = (pl.program_id(0) * z_ref.shape[0]) % seq
    _mix_rows(z_ref, halo_ref, pw_ref, ps_ref, lg_ref, lb_ref, sw_ref, sb_ref, y_ref, v_ref,
              pos0, wpool, wsgu)


def _mix(z, pool_w, pool_scale, ln_g, ln_b, sgu_w, sgu_b, *, layer, seq, rows):
    m, d_in = rows, z.shape[1]
    _, n_groups, gc, _ = pool_w.shape
    wpool = n_groups * gc
    wsgu = (d_in - wpool) // 2
    n_heads = wsgu // SGU_HEAD
    assert n_groups == len(POOL_WINDOWS) and sgu_w.shape[1:] == (n_heads, CHUNK, CHUNK)
    t = _tile(seq, ROW_TILE)
    assert t % CHUNK == 0 and t % POOL_HALO == 0
    halo_blocks = t // POOL_HALO
    sb = jnp.repeat(jnp.transpose(sgu_b), SGU_HEAD, axis=1)
    kern = functools.partial(_mix_kernel, seq=seq, wpool=wpool, wsgu=wsgu)
    const2 = lambda i: (0, 0)
    return pl.pallas_call(
        kern,
        grid=(m // t,),
        in_specs=[pl.BlockSpec((t, d_in), lambda i: (i, 0)),
                  pl.BlockSpec((POOL_HALO, wpool),
                               lambda i: (jnp.maximum(i * halo_blocks - 1, 0), 0)),
                  pl.BlockSpec((None, n_groups, gc, gc), lambda i: (layer, 0, 0, 0)),
                  pl.BlockSpec((1, wpool), const2),
                  pl.BlockSpec((1, wsgu), const2),
                  pl.BlockSpec((1, wsgu), const2),
                  pl.BlockSpec((None, n_heads, CHUNK, CHUNK), lambda i: (layer, 0, 0, 0)),
                  pl.BlockSpec((CHUNK, wsgu), const2)],
        out_specs=pl.BlockSpec((t, wpool + wsgu), lambda i: (i, 0)),
        out_shape=jax.ShapeDtypeStruct((m, wpool + wsgu), BF16),
        scratch_shapes=[pltpu.VMEM((t, wsgu), BF16)],
        compiler_params=_params("parallel"),
        name="mix",
    )(z, z, pool_w, pool_scale.reshape(1, wpool), ln_g.reshape(1, wsgu),
      ln_b.reshape(1, wsgu), sgu_w, sb)


def _fold_qk_kernel(wq_ref, k_ref, a_ref):
    n_mem = a_ref.shape[2]
    s = lax.dot_general(wq_ref[...].astype(BF16), k_ref[...], (((1,), (1,)), ((), ())),
                        preferred_element_type=F32)
    for b in range(a_ref.shape[0]):
        a_ref[b] = s[:, b * n_mem:(b + 1) * n_mem].astype(a_ref.dtype)


def _fold_vo_kernel(v_ref, wo_ref, b_ref):
    n_mem = b_ref.shape[1]
    s = jnp.dot(v_ref[...], wo_ref[...].astype(BF16), preferred_element_type=F32)
    for b in range(b_ref.shape[0]):
        b_ref[b] = s[b * n_mem:(b + 1) * n_mem].astype(b_ref.dtype)


def _fold_qk(wq, k, *, layer, n_batch, n_mem):
    d = wq.shape[1]
    dh = d // N_XHEADS
    rb = _tile(d, FOLD_TILE)
    return pl.pallas_call(
        _fold_qk_kernel,
        grid=(N_XHEADS, d // rb),
        in_specs=[pl.BlockSpec((None, rb, dh), lambda h, r: (layer, r, h)),
                  pl.BlockSpec((n_batch * n_mem, dh), lambda h, r: (0, h))],
        out_specs=pl.BlockSpec((n_batch, rb, n_mem), lambda h, r: (0, r, h)),
        out_shape=jax.ShapeDtypeStruct((n_batch, d, N_XHEADS * n_mem), BF16),
        compiler_params=_params("parallel", "parallel"),
        name="fold_qk",
    )(wq, k)


def _fold_vo(v, wo, *, layer, n_batch, n_mem):
    d = wo.shape[2]
    dh = d // N_XHEADS
    cb = _tile(d, FOLD_TILE)
    return pl.pallas_call(
        _fold_vo_kernel,
        grid=(N_XHEADS, d // cb),
        in_specs=[pl.BlockSpec((n_batch * n_mem, dh), lambda h, c: (0, h)),
                  pl.BlockSpec((None, dh, cb), lambda h, c: (layer, h, c))],
        out_specs=pl.BlockSpec((n_batch, n_mem, cb), lambda h, c: (0, h, c)),
        out_shape=jax.ShapeDtypeStruct((n_batch, N_XHEADS * n_mem, d), BF16),
        compiler_params=_params("parallel", "parallel"),
        name="fold_vo",
    )(v, wo)


def kernel(x, mem, ln_mix, w_in, pool_w, pool_scale, sgu_ln_g, sgu_ln_b, sgu_w, sgu_b,
           w_out, ln_x, ln_mem, w_q, w_k, w_v, w_o, ln_ffn, w_up, w_down, ln_final):
    n_batch, seq, d = x.shape
    n_mem = mem.shape[1]
    depth = w_in.shape[0]
    dh = d // N_XHEADS
    x = x.reshape(n_batch * seq, d)
    memf = mem.reshape(n_batch * n_mem, d)

    wb_in = w_in[:1].astype(BF16)
    pool_w, sgu_w = pool_w.astype(BF16), sgu_w.astype(BF16)

    blocked = lambda w, l: (w, l, min(MM_BN, w.shape[2]))

    for l in range(depth):
        z, (wb_out,) = _normin(x, ln_mix[l], wb_in, layer=0 if l == 0 else "blocked",
                               casts=[(w_out, l, _mixout_bn(seq, d))])
        x, xb, ssq, cast = _mixout(z, pool_w, pool_scale[l], sgu_ln_g[l], sgu_ln_b[l], sgu_w,
                                   sgu_b[l], wb_out, x, ln_x[l], layer=l, seq=seq,
                                   cast=blocked(w_up, 0) if l == 0 else None)
        if l == 0:
            wb_up = cast

        mb, mssq = _prep(memf, ln_mem[l])
        k, v = _mm_kv(mb, mssq, w_k, w_v, layer=l)
        a_fold = _fold_qk(w_q, k, layer=l, n_batch=n_batch, n_mem=n_mem)
        b_fold = _fold_vo(v, w_o, layer=l, n_batch=n_batch, n_mem=n_mem)
        p, _ = _mm(xb, a_fold, ssq=ssq, post_scale=dh ** -0.5, act="group_softmax",
                   n_groups=N_XHEADS, bn=N_XHEADS * n_mem, name="mm_scores")
        (x, xb, ssq), _ = _mm(p, b_fold, res=x, gain=ln_ffn[l], out_dtype=F32,
                              bm=ATTN_OUT_BM, bn=d, name="mm_attn_out")

        last = l + 1 == depth
        nxt = [] if last else [blocked(w_up, l + 1), blocked(w_in, l + 1)]
        hid, cast = _mm(xb, wb_up, layer="blocked", ssq=ssq, act="relu2",
                        casts=[blocked(w_down, l)] + nxt, name="mm_up")
        wb_down = cast[0]
        if not last:
            wb_up, wb_in = cast[1:]
        x, _ = _mm(hid, wb_down, layer="blocked", res=x, out_dtype=F32, bk=MM_BK_DOWN,
                   name="mm_down")

    return _final_norm(x, ln_final).reshape(n_batch, seq, d)
```
